```python
import math
import jax, jax.numpy as jnp
from jax import lax
import numpy as np

D_MODEL = 4096
BATCH = 4
SEQ = 2048
DEPTH = 2
DEC_BATCH = 8
DEC_SEQ = 1
PAST_LEN = 16384
PAGE_SIZE = 128

N_AB_LAYERS = (DEPTH + 1) // 2
N_C_LAYERS = DEPTH // 2

A_HEADS = 16
A_DK = 128
A_DV = 128
A_CONV = 4
A_CHUNK = 64
A_QK_W = A_HEADS * A_DK
A_V_W = A_HEADS * A_DV
A_CONV_DIM = 2 * A_QK_W + A_V_W
P_A = A_CONV_DIM + A_V_W + 2 * A_HEADS

B_HEADS = 16
B_DH = 128
B_W = B_HEADS * B_DH
MOBA_BLOCK = 256
MOBA_TOPK = 3
ROPE_THETA = 500000.0
ROPE_DIM = B_DH // 4
Q_BLOCK = 128
P_AB = P_A + 3 * B_W

C_WIDTH = D_MODEL
C_GROUPS = 16
C_GROUP_W = C_WIDTH // C_GROUPS
C_CHUNK = 128

PEER_HEADS = 8
PEER_NKEYS = 128
PEER_EXPERTS = PEER_NKEYS * PEER_NKEYS
PEER_QDIM = 256
PEER_TOPK = 16
PEER_TOK_BLOCK = 128

NORM_EPS = 1e-6
LN_EPS = 1e-5

kernel_name = 'hybrid_deltanet_moba_sgu_peer_step'


def rms_norm(x, g):
    xf = x.astype(jnp.float32)
    y = xf * lax.rsqrt(jnp.mean(xf * xf, axis=-1, keepdims=True) + NORM_EPS)
    return (y * g.astype(jnp.float32)).astype(x.dtype)


def layer_norm(x, g, b):
    xf = x.astype(jnp.float32)
    mu = jnp.mean(xf, axis=-1, keepdims=True)
    xc = xf - mu
    y = xc * lax.rsqrt(jnp.mean(xc * xc, axis=-1, keepdims=True) + LN_EPS)
    return (y * g.astype(jnp.float32) + b.astype(jnp.float32)).astype(x.dtype)


def l2_norm(x):
    xf = x.astype(jnp.float32)
    return xf * lax.rsqrt(jnp.sum(xf * xf, axis=-1, keepdims=True) + NORM_EPS)


def partial_rope(x, pos):
    half = ROPE_DIM // 2
    inv_freq = ROPE_THETA ** (-jnp.arange(half, dtype=jnp.float32) / half)
    ang = pos.astype(jnp.float32)[:, None] * inv_freq[None, :]
    cos = jnp.cos(ang)[:, None, :]
    sin = jnp.sin(ang)[:, None, :]
    xr = x[..., :ROPE_DIM].astype(jnp.float32)
    x1, x2 = xr[..., :half], xr[..., half:]
    rot = jnp.concatenate([x1 * cos - x2 * sin, x2 * cos + x1 * sin], axis=-1)
    return jnp.concatenate([rot.astype(x.dtype), x[..., ROPE_DIM:]], axis=-1)


def gated_delta_chunked(q, k, v, g, beta, s0):
    bsz, L, H, dk = q.shape
    dv = v.shape[-1]
    c = min(A_CHUNK, L)
    n = -(-L // c)
    pad = n * c - L

    def prep(t):
        t = t.astype(jnp.float32)
        t = jnp.pad(t, [(0, 0), (0, pad)] + [(0, 0)] * (t.ndim - 2))
        t = t.reshape((bsz, n, c) + t.shape[2:])
        return jnp.moveaxis(t, (1, 2), (0, 3))

    qc = prep(q) * dk ** -0.5
    kc, vc, gc, bc = prep(k), prep(v), prep(g), prep(beta)
    G = jnp.cumsum(gc, axis=-1)
    tri = jnp.tril(jnp.ones((c, c), bool))
    strict_tri = jnp.tril(jnp.ones((c, c), bool), -1)
    diff = G[..., :, None] - G[..., None, :]
    decay = jnp.where(tri, jnp.exp(jnp.where(tri, diff, 0.0)), 0.0)
    kbeta = kc * bc[..., None]
    lower = jnp.where(strict_tri, jnp.einsum('nbhid,nbhjd->nbhij', kbeta, kc) * decay, 0.0)
    eye = jnp.eye(c, dtype=jnp.float32)
    t_inv = lax.linalg.triangular_solve(eye + lower, jnp.broadcast_to(eye, lower.shape),
                                        left_side=True, lower=True, unit_diagonal=True)
    u = t_inv @ (vc * bc[..., None])
    w = t_inv @ (kbeta * jnp.exp(G)[..., None])
    qk = jnp.where(tri, jnp.einsum('nbhid,nbhjd->nbhij', qc, kc) * decay, 0.0)

    def step(S, xs):
        qi, ki, ui, wi, Gi, qki = xs
        v_new = ui - wi @ S
        o = (qi * jnp.exp(Gi)[..., None]) @ S + qki @ v_new
        g_last = Gi[..., -1]
        S = S * jnp.exp(g_last)[..., None, None] + jnp.einsum(
            'bhcd,bhce->bhde', ki * jnp.exp(g_last[..., None] - Gi)[..., None], v_new)
        return S, o

    S, o = lax.scan(step, s0.astype(jnp.float32), (qc, kc, u, w, G, qk))
    o = jnp.moveaxis(o, (0, 3), (1, 2)).reshape(bsz, n * c, H, dv)[:, :L]
    return o, S.astype(s0.dtype)


def mixer_a(pa, conv_buf, s0, conv_w, a_log, dt_bias, out_norm):
    bsz, L, _ = pa.shape
    qkv = pa[..., :A_CONV_DIM]
    z = pa[..., A_CONV_DIM:A_CONV_DIM + A_V_W]
    b_logit = pa[..., A_CONV_DIM + A_V_W:A_CONV_DIM + A_V_W + A_HEADS]
    a_in = pa[..., A_CONV_DIM + A_V_W + A_HEADS:]
    xp = jnp.concatenate([conv_buf.astype(qkv.dtype), qkv], axis=1)
    conv = jax.nn.silu(sum(xp[:, j:j + L] * conv_w[:, j] for j in range(A_CONV)))
    new_buf = xp[:, L:]
    q = l2_norm(conv[..., :A_QK_W].reshape(bsz, L, A_HEADS, A_DK))
    k = l2_norm(conv[..., A_QK_W:2 * A_QK_W].reshape(bsz, L, A_HEADS, A_DK))
    v = conv[..., 2 * A_QK_W:].reshape(bsz, L, A_HEADS, A_DV)
    beta = jax.nn.sigmoid(b_logit.astype(jnp.float32))
    g = -jnp.exp(a_log.astype(jnp.float32)) * jax.nn.softplus(a_in.astype(jnp.float32) + dt_bias.astype(jnp.float32))
    o, s_new = gated_delta_chunked(q, k, v, g, beta, s0)
    o = rms_norm(o, out_norm) * jax.nn.silu(z.reshape(bsz, L, A_HEADS, A_DV).astype(jnp.float32))
    return o.reshape(bsz, L, A_V_W).astype(pa.dtype), new_buf, s_new


def moba_attend_block(q, q_pos, kb, vb, kmean):
    n_blocks = kb.shape[0]
    n_heads = q.shape[1]
    scale = q.shape[-1] ** -0.5
    qf = q.astype(jnp.float32)
    own = q_pos[0] // MOBA_BLOCK
    offs = jnp.arange(MOBA_BLOCK, dtype=jnp.int32)
    ko = lax.dynamic_index_in_dim(kb, own, 0, keepdims=False).astype(jnp.float32)
    vo = lax.dynamic_index_in_dim(vb, own, 0, keepdims=False).astype(jnp.float32)
    s = jnp.einsum('qhd,khd->qhk', qf, ko) * scale
    s = jnp.where((own * MOBA_BLOCK + offs)[None, None, :] <= q_pos[:, None, None], s, -jnp.inf)
    m = jnp.max(s, axis=-1)
    p = jnp.exp(s - m[..., None])
    l = jnp.sum(p, axis=-1)
    acc = jnp.einsum('qhk,khd->qhd', p, vo)
    gate = jnp.einsum('qhd,nhd->qhn', qf, kmean)
    gate = jnp.where(jnp.arange(n_blocks)[None, None, :] < own, gate, -jnp.inf)
    _, sel = lax.top_k(gate, min(MOBA_TOPK, n_blocks))
    hid = jnp.arange(n_heads)[None, :]
    for slot in range(sel.shape[-1]):
        blk = sel[..., slot]
        ks = kb[blk, :, hid, :].astype(jnp.float32)
        vs = vb[blk, :, hid, :].astype(jnp.float32)
        s = jnp.einsum('qhd,qhkd->qhk', qf, ks) * scale
        s = jnp.where((blk < own)[..., None], s, -jnp.inf)
        m_new = jnp.maximum(m, jnp.max(s, axis=-1))
        corr = jnp.exp(m - m_new)
        p = jnp.exp(s - m_new[..., None])
        l = l * corr + jnp.sum(p, axis=-1)
        acc = acc * corr[..., None] + jnp.einsum('qhk,qhkd->qhd', p, vs)
        m = m_new
    return (acc / l[..., None]).astype(q.dtype)


def moba_attention(q, pos0, k_parts, v_parts):
    bsz, L, H, dh = q.shape
    qb = min(Q_BLOCK, L)
    nq = -(-L // qb)
    lp = nq * qb
    T = sum(t.shape[1] for t in k_parts)
    n_blocks = -(-(pos0 + lp) // MOBA_BLOCK)
    pad = jnp.zeros((bsz, n_blocks * MOBA_BLOCK - T, H, dh), q.dtype)
    kb = jnp.concatenate(k_parts + [pad], axis=1).reshape(bsz, n_blocks, MOBA_BLOCK, H, dh)
    vb = jnp.concatenate(v_parts + [pad], axis=1).reshape(bsz, n_blocks, MOBA_BLOCK, H, dh)
    kmean = jnp.mean(kb, axis=2, dtype=jnp.float32)
    qp = jnp.pad(q, ((0, 0), (0, lp - L), (0, 0), (0, 0))).reshape(bsz * nq, qb, H, dh)
    pos = jnp.tile((pos0 + jnp.arange(lp, dtype=jnp.int32)).reshape(nq, qb), (bsz, 1))
    bidx = jnp.repeat(jnp.arange(bsz, dtype=jnp.int32), nq)

    def one(args):
        qq, pp, bi = args
        return moba_attend_block(qq, pp, kb[bi], vb[bi], kmean[bi])

    out = lax.map(one, (qp, pos, bidx))
    return out.reshape(bsz, lp, H, dh)[:, :L]


def ab_mixer(h, pos0, conv_buf, s0, k_past, v_past, w_in, w_out, conv_w, a_log, dt_bias, out_norm, q_norm, k_norm):
    bsz, L, _ = h.shape
    p = h @ w_in
    oa, conv_new, s_new = mixer_a(p[..., :P_A], conv_buf, s0, conv_w, a_log, dt_bias, out_norm)
    pb = p[..., P_A:]
    pos = pos0 + jnp.arange(L, dtype=jnp.int32)
    q = partial_rope(rms_norm(pb[..., :B_W].reshape(bsz, L, B_HEADS, B_DH), q_norm), pos)
    k = partial_rope(rms_norm(pb[..., B_W:2 * B_W].reshape(bsz, L, B_HEADS, B_DH), k_norm), pos)
    v = pb[..., 2 * B_W:].reshape(bsz, L, B_HEADS, B_DH)
    ob = moba_attention(q, pos0, k_past + [k], v_past + [v])
    y = jnp.concatenate([oa, ob.reshape(bsz, L, B_W).astype(oa.dtype)], axis=-1) @ w_out
    return y, k, v, conv_new, s_new


def c_mixer(h, w_in, b_in, ln_g, ln_b, w_s, b_s, w_out):
    bsz, L, _ = h.shape
    z = jax.nn.gelu(h @ w_in + b_in, approximate=False)
    u, v = z[..., :C_WIDTH], z[..., C_WIDTH:]
    v = layer_norm(v, ln_g, ln_b)
    c = min(C_CHUNK, L)
    n = -(-L // c)
    lp = n * c
    vp = jnp.pad(v, ((0, 0), (0, lp - L), (0, 0))).reshape(bsz, n, c, C_GROUPS, C_GROUP_W)
    ws = jnp.tril(w_s[:, :c, :c])
    mix = jnp.einsum('gij,bnjgc->bnigc', ws, vp) + b_s[:, :c].T[None, None, :, :, None]
    mix = mix.reshape(bsz, lp, C_WIDTH)[:, :L]
    return (u * mix) @ w_out, v


def peer_block(x, w_q, sub_keys, u_tab, v_tab):
    T = x.shape[0]
    q = (x @ w_q).astype(jnp.float32).reshape(T, PEER_HEADS, 2, PEER_QDIM // 2)
    s = jnp.einsum('thpd,pkd->thpk', q, sub_keys.astype(jnp.float32))
    s1, i1 = lax.top_k(s[:, :, 0], PEER_TOPK)
    s2, i2 = lax.top_k(s[:, :, 1], PEER_TOPK)
    cand = (s1[..., :, None] + s2[..., None, :]).reshape(T, PEER_HEADS, PEER_TOPK * PEER_TOPK)
    cid = (i1[..., :, None] * PEER_NKEYS + i2[..., None, :]).reshape(T, PEER_HEADS, PEER_TOPK * PEER_TOPK)
    sc, j = lax.top_k(cand, PEER_TOPK)
    eid = jnp.take_along_axis(cid, j, axis=-1)
    gate = jax.nn.softmax(sc, axis=-1)
    act = jax.nn.gelu(jnp.einsum('td,thkd->thk', x.astype(jnp.float32), u_tab[eid].astype(jnp.float32)), approximate=False)
    out = jnp.einsum('thk,thkd->td', gate * act, v_tab[eid].astype(jnp.float32))
    return out.astype(x.dtype)


def peer(x, w_q, sub_keys, u_tab, v_tab):
    shp = x.shape
    xf = x.reshape(-1, shp[-1])
    n = xf.shape[0]
    blk = min(PEER_TOK_BLOCK, n)
    pad = -(-n // blk) * blk - n
    xs = jnp.pad(xf, ((0, pad), (0, 0))).reshape(-1, blk, shp[-1])
    out = lax.map(lambda t: peer_block(t, w_q, sub_keys, u_tab, v_tab), xs)
    return out.reshape(-1, shp[-1])[:n].reshape(shp)


def setup_inputs(seed: int = 0) -> dict:
    key = jax.random.key(seed)
    k = jax.random.split(key, 28)
    f32 = jnp.float32

    def nrm(kk, shape, scale):
        return jax.random.normal(kk, shape, f32) * scale

    n_pages = PAST_LEN // PAGE_SIZE
    n_used = DEC_BATCH * n_pages
    n_phys = n_used + n_used // 4
    page_table = jax.random.permutation(k[6], n_phys)[:n_used].reshape(DEC_BATCH, n_pages).astype(jnp.int32)
    dt = jnp.exp(jax.random.uniform(k[12], (N_AB_LAYERS, A_HEADS), f32, math.log(1e-3), math.log(1e-1)))
    return {
        'x_prompt': nrm(k[0], (BATCH, SEQ, D_MODEL), 1.0),
        'x_sample': nrm(k[1], (DEC_BATCH, DEC_SEQ, D_MODEL), 1.0),
        'cache_k': nrm(k[2], (N_AB_LAYERS, n_phys, PAGE_SIZE, B_HEADS, B_DH), 1.0),
        'cache_v': nrm(k[3], (N_AB_LAYERS, n_phys, PAGE_SIZE, B_HEADS, B_DH), 1.0),
        'state_delta': nrm(k[4], (N_AB_LAYERS, DEC_BATCH, A_HEADS, A_DK, A_DV), 0.1),
        'state_conv': nrm(k[5], (N_AB_LAYERS, DEC_BATCH, A_CONV - 1, A_CONV_DIM), 1.0),
        'page_table': page_table,
        'norm_mix': 1.0 + nrm(k[7], (DEPTH, D_MODEL), 0.02),
        'norm_ffn': 1.0 + nrm(k[8], (DEPTH, D_MODEL), 0.02),
        'ab_w_in': nrm(k[9], (N_AB_LAYERS, D_MODEL, P_AB), D_MODEL ** -0.5),
        'ab_w_out': nrm(k[10], (N_AB_LAYERS, A_V_W + B_W, D_MODEL), (A_V_W + B_W) ** -0.5),
        'a_conv_w': nrm(k[11], (N_AB_LAYERS, A_CONV_DIM, A_CONV), A_CONV ** -0.5),
        'a_log': jnp.log(jax.random.uniform(k[13], (N_AB_LAYERS, A_HEADS), f32, 1.0, 16.0)),
        'a_dt_bias': dt + jnp.log(-jnp.expm1(-dt)),
        'a_out_norm': 1.0 + nrm(k[14], (N_AB_LAYERS, A_DV), 0.02),
        'b_q_norm': 1.0 + nrm(k[15], (N_AB_LAYERS, B_DH), 0.02),
        'b_k_norm': 1.0 + nrm(k[16], (N_AB_LAYERS, B_DH), 0.02),
        'c_w_in': nrm(k[17], (N_C_LAYERS, D_MODEL, 2 * C_WIDTH), D_MODEL ** -0.5),
        'c_b_in': nrm(k[18], (N_C_LAYERS, 2 * C_WIDTH), 0.02),
        'c_ln_g': 1.0 + nrm(k[19], (N_C_LAYERS, C_WIDTH), 0.02),
        'c_ln_b': nrm(k[20], (N_C_LAYERS, C_WIDTH), 0.02),
        'c_w_s': nrm(k[21], (N_C_LAYERS, C_GROUPS, C_CHUNK, C_CHUNK), C_CHUNK ** -0.5),
        'c_b_s': 1.0 + nrm(k[22], (N_C_LAYERS, C_GROUPS, C_CHUNK), 0.02),
        'c_w_out': nrm(k[23], (N_C_LAYERS, C_WIDTH, D_MODEL), C_WIDTH ** -0.5),
        'peer_w_q': nrm(k[24], (DEPTH, D_MODEL, PEER_HEADS * PEER_QDIM), D_MODEL ** -0.5),
        'peer_sub_keys': nrm(k[25], (DEPTH, 2, PEER_NKEYS, PEER_QDIM // 2), (PEER_QDIM // 2) ** -0.5),
        'peer_u': nrm(k[26], (DEPTH, PEER_EXPERTS, D_MODEL), D_MODEL ** -0.5),
        'peer_v': nrm(k[27], (DEPTH, PEER_EXPERTS, D_MODEL), (PEER_HEADS * PEER_TOPK) ** -0.5),
    }


def reference(x_prompt, x_sample, cache_k, cache_v, state_delta, state_conv, page_table,
              norm_mix, norm_ffn, ab_w_in, ab_w_out, a_conv_w, a_log, a_dt_bias, a_out_norm,
              b_q_norm, b_k_norm, c_w_in, c_b_in, c_ln_g, c_ln_b, c_w_s, c_b_s, c_w_out,
              peer_w_q, peer_sub_keys, peer_u, peer_v):
    bp = x_prompt.shape[0]
    bs = x_sample.shape[0]
    past_len = page_table.shape[1] * cache_k.shape[2]
    yp, ys = x_prompt, x_sample
    kp_l, vp_l, dp_l, cp_l = [], [], [], []
    ks_l, vs_l, ds_l, cs_l = [], [], [], []
    sv_l = []
    for layer in range(DEPTH):
        hp = rms_norm(yp, norm_mix[layer])
        hs = rms_norm(ys, norm_mix[layer])
        if layer % 2 == 0:
            i = layer // 2
            w = (ab_w_in[i], ab_w_out[i], a_conv_w[i], a_log[i], a_dt_bias[i], a_out_norm[i], b_q_norm[i], b_k_norm[i])
            k_past = cache_k[i, page_table].reshape(bs, past_len, B_HEADS, B_DH)
            v_past = cache_v[i, page_table].reshape(bs, past_len, B_HEADS, B_DH)
            conv0 = jnp.zeros((bp, A_CONV - 1, A_CONV_DIM), x_prompt.dtype)
            s0 = jnp.zeros((bp, A_HEADS, A_DK, A_DV), jnp.float32)
            mp, kp, vp, cp, dp = ab_mixer(hp, 0, conv0, s0, [], [], *w)
            ms, kn, vn, cn, dn = ab_mixer(hs, past_len, state_conv[i], state_delta[i], [k_past], [v_past], *w)
            kp_l.append(kp); vp_l.append(vp); dp_l.append(dp); cp_l.append(cp)
            ks_l.append(kn); vs_l.append(vn); ds_l.append(dn); cs_l.append(cn)
        else:
            j = layer // 2
            w = (c_w_in[j], c_b_in[j], c_ln_g[j], c_ln_b[j], c_w_s[j], c_b_s[j], c_w_out[j])
            mp, _ = c_mixer(hp, *w)
            ms, sv = c_mixer(hs, *w)
            sv_l.append(sv)
        yp = yp + mp
        ys = ys + ms
        pw = (peer_w_q[layer], peer_sub_keys[layer], peer_u[layer], peer_v[layer])
        yp = yp + peer(rms_norm(yp, norm_ffn[layer]), *pw)
        ys = ys + peer(rms_norm(ys, norm_ffn[layer]), *pw)
    return (yp, ys, jnp.stack(kp_l), jnp.stack(vp_l), jnp.stack(dp_l), jnp.stack(cp_l),
            jnp.stack(ks_l), jnp.stack(vs_l), jnp.stack(ds_l), jnp.stack(cs_l), jnp.stack(sv_l))
```

```python
import functools
import math

import jax
import jax.numpy as jnp
from jax import lax
from jax.experimental import pallas as pl
from jax.experimental.pallas import tpu as pltpu

D_MODEL = 4096
DEPTH = 2

A_HEADS = 16
A_DK = 128
A_DV = 128
A_CONV = 4
A_CHUNK = 64
A_QK_W = A_HEADS * A_DK
A_V_W = A_HEADS * A_DV
A_CONV_DIM = 2 * A_QK_W + A_V_W
P_A = A_CONV_DIM + A_V_W + 2 * A_HEADS

B_HEADS = 16
B_DH = 128
B_W = B_HEADS * B_DH
MOBA_BLOCK = 256
MOBA_TOPK = 3
ROPE_THETA = 500000.0
ROPE_DIM = B_DH // 4
Q_BLOCK = 128

C_WIDTH = D_MODEL
C_GROUPS = 16
C_GROUP_W = C_WIDTH // C_GROUPS
C_CHUNK = 128

PEER_HEADS = 8
PEER_NKEYS = 128
PEER_QDIM = 256
PEER_TOPK = 16
PEER_TOK_BLOCK = 128

NORM_EPS = 1e-6
LN_EPS = 1e-5

V7X_VMEM_LIMIT_BYTES = 48 * 1024 * 1024


def _mm_kernel(a_ref, b_ref, o_ref):
    o_ref[...] = jnp.dot(a_ref[...], b_ref[...], preferred_element_type=jnp.float32)


def _mm_res_kernel(a_ref, b_ref, r_ref, o_ref):
    o_ref[...] = r_ref[...] + jnp.dot(a_ref[...], b_ref[...], preferred_element_type=jnp.float32)


def _matmul(a, b, residual=None):
    m, k = a.shape
    n = b.shape[1]
    tm = min(m, 512)
    tn = min(n, 1024)
    assert m % tm == 0 and n % tn == 0, (m, n)
    grid = (n // tn, m // tm)
    in_specs = [pl.BlockSpec((tm, k), lambda j, i: (i, 0)),
                pl.BlockSpec((k, tn), lambda j, i: (0, j))]
    args = [a, b]
    body = _mm_kernel
    if residual is not None:
        in_specs.append(pl.BlockSpec((tm, tn), lambda j, i: (i, j)))
        args.append(residual)
        body = _mm_res_kernel
    return pl.pallas_call(
        body,
        grid=grid,
        in_specs=in_specs,
        out_specs=pl.BlockSpec((tm, tn), lambda j, i: (i, j)),
        out_shape=jax.ShapeDtypeStruct((m, n), jnp.float32),
        compiler_params=pltpu.CompilerParams(
            dimension_semantics=("arbitrary", "arbitrary"),
            vmem_limit_bytes=V7X_VMEM_LIMIT_BYTES),
        name="matmul",
    )(*args)


def _rms_kernel(x_ref, g_ref, o_ref):
    x = x_ref[...]
    y = x * lax.rsqrt(jnp.mean(x * x, axis=-1, keepdims=True) + NORM_EPS)
    o_ref[...] = (y * g_ref[...]).astype(o_ref.dtype)


def _rmsnorm_bf16(x, g):
    t, d = x.shape
    tr = min(t, 256)
    assert t % tr == 0
    return pl.pallas_call(
        _rms_kernel,
        grid=(t // tr,),
        in_specs=[pl.BlockSpec((tr, d), lambda i: (i, 0)),
                  pl.BlockSpec((1, d), lambda i: (0, 0))],
        out_specs=pl.BlockSpec((tr, d), lambda i: (i, 0)),
        out_shape=jax.ShapeDtypeStruct((t, d), jnp.bfloat16),
        compiler_params=pltpu.CompilerParams(dimension_semantics=("arbitrary",)),
        name="rmsnorm",
    )(x, g.reshape(1, d))


def rms_norm(x, g):
    xf = x.astype(jnp.float32)
    y = xf * lax.rsqrt(jnp.mean(xf * xf, axis=-1, keepdims=True) + NORM_EPS)
    return (y * g.astype(jnp.float32)).astype(x.dtype)


def layer_norm(x, g, b):
    xf = x.astype(jnp.float32)
    mu = jnp.mean(xf, axis=-1, keepdims=True)
    xc = xf - mu
    y = xc * lax.rsqrt(jnp.mean(xc * xc, axis=-1, keepdims=True) + LN_EPS)
    return (y * g.astype(jnp.float32) + b.astype(jnp.float32)).astype(x.dtype)


def l2_norm(x):
    xf = x.astype(jnp.float32)
    return xf * lax.rsqrt(jnp.sum(xf * xf, axis=-1, keepdims=True) + NORM_EPS)


def partial_rope(x, pos):
    half = ROPE_DIM // 2
    inv_freq = ROPE_THETA ** (-jnp.arange(half, dtype=jnp.float32) / half)
    ang = pos.astype(jnp.float32)[:, None] * inv_freq[None, :]
    cos = jnp.cos(ang)[:, None, :]
    sin = jnp.sin(ang)[:, None, :]
    xr = x[..., :ROPE_DIM].astype(jnp.float32)
    x1, x2 = xr[..., :half], xr[..., half:]
    rot = jnp.concatenate([x1 * cos - x2 * sin, x2 * cos + x1 * sin], axis=-1)
    return jnp.concatenate([rot.astype(x.dtype), x[..., ROPE_DIM:]], axis=-1)


def gated_delta_chunked(q, k, v, g, beta, s0):
    bsz, L, H, dk = q.shape
    dv = v.shape[-1]
    c = min(A_CHUNK, L)
    n = -(-L // c)
    pad = n * c - L

    def prep(t):
        t = t.astype(jnp.float32)
        t = jnp.pad(t, [(0, 0), (0, pad)] + [(0, 0)] * (t.ndim - 2))
        t = t.reshape((bsz, n, c) + t.shape[2:])
        return jnp.moveaxis(t, (1, 2), (0, 3))

    qc = prep(q) * dk ** -0.5
    kc, vc, gc, bc = prep(k), prep(v), prep(g), prep(beta)
    G = jnp.cumsum(gc, axis=-1)
    tri = jnp.tril(jnp.ones((c, c), bool))
    strict_tri = jnp.tril(jnp.ones((c, c), bool), -1)
    diff = G[..., :, None] - G[..., None, :]
    decay = jnp.where(tri, jnp.exp(jnp.where(tri, diff, 0.0)), 0.0)
    kbeta = kc * bc[..., None]
    lower = jnp.where(strict_tri, jnp.einsum('nbhid,nbhjd->nbhij', kbeta, kc) * decay, 0.0)
    eye = jnp.eye(c, dtype=jnp.float32)
    t_inv = lax.linalg.triangular_solve(eye + lower, jnp.broadcast_to(eye, lower.shape),
                                        left_side=True, lower=True, unit_diagonal=True)
    u = t_inv @ (vc * bc[..., None])
    w = t_inv @ (kbeta * jnp.exp(G)[..., None])
    qk = jnp.where(tri, jnp.einsum('nbhid,nbhjd->nbhij', qc, kc) * decay, 0.0)

    def step(S, xs):
        qi, ki, ui, wi, Gi, qki = xs
        v_new = ui - wi @ S
        o = (qi * jnp.exp(Gi)[..., None]) @ S + qki @ v_new
        g_last = Gi[..., -1]
        S = S * jnp.exp(g_last)[..., None, None] + jnp.einsum(
            'bhcd,bhce->bhde', ki * jnp.exp(g_last[..., None] - Gi)[..., None], v_new)
        return S, o

    S, o = lax.scan(step, s0.astype(jnp.float32), (qc, kc, u, w, G, qk))
    o = jnp.moveaxis(o, (0, 3), (1, 2)).reshape(bsz, n * c, H, dv)[:, :L]
    return o, S.astype(s0.dtype)


def mixer_a(pa, conv_buf, s0, conv_w, a_log, dt_bias, out_norm):
    bsz, L, _ = pa.shape
    qkv = pa[..., :A_CONV_DIM]
    z = pa[..., A_CONV_DIM:A_CONV_DIM + A_V_W]
    b_logit = pa[..., A_CONV_DIM + A_V_W:A_CONV_DIM + A_V_W + A_HEADS]
    a_in = pa[..., A_CONV_DIM + A_V_W + A_HEADS:]
    xp = jnp.concatenate([conv_buf.astype(qkv.dtype), qkv], axis=1)
    conv = jax.nn.silu(sum(xp[:, j:j + L] * conv_w[:, j] for j in range(A_CONV)))
    new_buf = xp[:, L:]
    q = l2_norm(conv[..., :A_QK_W].reshape(bsz, L, A_HEADS, A_DK))
    k = l2_norm(conv[..., A_QK_W:2 * A_QK_W].reshape(bsz, L, A_HEADS, A_DK))
    v = conv[..., 2 * A_QK_W:].reshape(bsz, L, A_HEADS, A_DV)
    beta = jax.nn.sigmoid(b_logit.astype(jnp.float32))
    g = -jnp.exp(a_log.astype(jnp.float32)) * jax.nn.softplus(a_in.astype(jnp.float32) + dt_bias.astype(jnp.float32))
    o, s_new = gated_delta_chunked(q, k, v, g, beta, s0)
    o = rms_norm(o, out_norm) * jax.nn.silu(z.reshape(bsz, L, A_HEADS, A_DV).astype(jnp.float32))
    return o.reshape(bsz, L, A_V_W).astype(pa.dtype), new_buf, s_new


def moba_attend_block(q, q_pos, kb, vb, kmean):
    n_blocks = kb.shape[0]
    n_heads = q.shape[1]
    scale = q.shape[-1] ** -0.5
    qf = q.astype(jnp.float32)
    own = q_pos[0] // MOBA_BLOCK
    offs = jnp.arange(MOBA_BLOCK, dtype=jnp.int32)
    ko = lax.dynamic_index_in_dim(kb, own, 0, keepdims=False).astype(jnp.float32)
    vo = lax.dynamic_index_in_dim(vb, own, 0, keepdims=False).astype(jnp.float32)
    s = jnp.einsum('qhd,khd->qhk', qf, ko) * scale
    s = jnp.where((own * MOBA_BLOCK + offs)[None, None, :] <= q_pos[:, None, None], s, -jnp.inf)
    m = jnp.max(s, axis=-1)
    p = jnp.exp(s - m[..., None])
    l = jnp.sum(p, axis=-1)
    acc = jnp.einsum('qhk,khd->qhd', p, vo)
    gate = jnp.einsum('qhd,nhd->qhn', qf, kmean)
    gate = jnp.where(jnp.arange(n_blocks)[None, None, :] < own, gate, -jnp.inf)
    _, sel = lax.top_k(gate, min(MOBA_TOPK, n_blocks))
    hid = jnp.arange(n_heads)[None, :]
    for slot in range(sel.shape[-1]):
        blk = sel[..., slot]
        ks = kb[blk, :, hid, :].astype(jnp.float32)
        vs = vb[blk, :, hid, :].astype(jnp.float32)
        s = jnp.einsum('qhd,qhkd->qhk', qf, ks) * scale
        s = jnp.where((blk < own)[..., None], s, -jnp.inf)
        m_new = jnp.maximum(m, jnp.max(s, axis=-1))
        corr = jnp.exp(m - m_new)
        p = jnp.exp(s - m_new[..., None])
        l = l * corr + jnp.sum(p, axis=-1)
        acc = acc * corr[..., None] + jnp.einsum('qhk,qhkd->qhd', p, vs)
        m = m_new
    return (acc / l[..., None]).astype(q.dtype)


def moba_attention(q, pos0, k_parts, v_parts):
    bsz, L, H, dh = q.shape
    qb = min(Q_BLOCK, L)
    nq = -(-L // qb)
    lp = nq * qb
    T = sum(t.shape[1] for t in k_parts)
    n_blocks = -(-(pos0 + lp) // MOBA_BLOCK)
    pad = jnp.zeros((bsz, n_blocks * MOBA_BLOCK - T, H, dh), q.dtype)
    kb = jnp.concatenate(k_parts + [pad], axis=1).reshape(bsz, n_blocks, MOBA_BLOCK, H, dh)
    vb = jnp.concatenate(v_parts + [pad], axis=1).reshape(bsz, n_blocks, MOBA_BLOCK, H, dh)
    kmean = jnp.mean(kb, axis=2, dtype=jnp.float32)
    qp = jnp.pad(q, ((0, 0), (0, lp - L), (0, 0), (0, 0))).reshape(bsz * nq, qb, H, dh)
    pos = jnp.tile((pos0 + jnp.arange(lp, dtype=jnp.int32)).reshape(nq, qb), (bsz, 1))
    bidx = jnp.repeat(jnp.arange(bsz, dtype=jnp.int32), nq)

    def one(args):
        qq, pp, bi = args
        return moba_attend_block(qq, pp, kb[bi], vb[bi], kmean[bi])

    out = lax.map(one, (qp, pos, bidx))
    return out.reshape(bsz, lp, H, dh)[:, :L]


def _project(h, w_bf16):
    bsz, L, d = h.shape
    return _matmul(h.reshape(bsz * L, d).astype(jnp.bfloat16), w_bf16).reshape(bsz, L, -1)


def ab_mixer(h, pos0, conv_buf, s0, k_past, v_past, w_in, w_out, conv_w, a_log, dt_bias, out_norm, q_norm, k_norm):
    bsz, L, _ = h.shape
    w_main, w_small, w_b = w_in
    p_main = _project(h, w_main)
    p_small = _project(h, w_small)[..., :2 * A_HEADS]
    pb = _project(h, w_b)
    pa = jnp.concatenate([p_main, p_small], axis=-1)
    oa, conv_new, s_new = mixer_a(pa, conv_buf, s0, conv_w, a_log, dt_bias, out_norm)
    pos = pos0 + jnp.arange(L, dtype=jnp.int32)
    q = partial_rope(rms_norm(pb[..., :B_W].reshape(bsz, L, B_HEADS, B_DH), q_norm), pos)
    k = partial_rope(rms_norm(pb[..., B_W:2 * B_W].reshape(bsz, L, B_HEADS, B_DH), k_norm), pos)
    v = pb[..., 2 * B_W:].reshape(bsz, L, B_HEADS, B_DH)
    ob = moba_attention(q, pos0, k_past + [k], v_past + [v])
    cat = jnp.concatenate([oa, ob.reshape(bsz, L, B_W).astype(oa.dtype)], axis=-1)
    y = _project(cat, w_out)
    return y, k, v, conv_new, s_new


def c_mixer(h, w_in, b_in, ln_g, ln_b, w_s, b_s, w_out):
    bsz, L, _ = h.shape
    z = jax.nn.gelu(_project(h, w_in) + b_in, approximate=False)
    u, v = z[..., :C_WIDTH], z[..., C_WIDTH:]
    v = layer_norm(v, ln_g, ln_b)
    c = min(C_CHUNK, L)
    n = -(-L // c)
    lp = n * c
    vp = jnp.pad(v, ((0, 0), (0, lp - L), (0, 0))).reshape(bsz, n, c, C_GROUPS, C_GROUP_W)
    ws = jnp.tril(w_s[:, :c, :c])
    mix = jnp.einsum('gij,bnjgc->bnigc', ws, vp) + b_s[:, :c].T[None, None, :, :, None]
    mix = mix.reshape(bsz, lp, C_WIDTH)[:, :L]
    return _project(u * mix, w_out), v


def peer_block(x, w_q, sub_keys, u_tab, v_tab):
    T = x.shape[0]
    q = _matmul(x.astype(jnp.bfloat16), w_q).reshape(T, PEER_HEADS, 2, PEER_QDIM // 2)
    s = jnp.einsum('thpd,pkd->thpk', q, sub_keys.astype(jnp.float32))
    s1, i1 = lax.top_k(s[:, :, 0], PEER_TOPK)
    s2, i2 = lax.top_k(s[:, :, 1], PEER_TOPK)
    cand = (s1[..., :, None] + s2[..., None, :]).reshape(T, PEER_HEADS, PEER_TOPK * PEER_TOPK)
    cid = (i1[..., :, None] * PEER_NKEYS + i2[..., None, :]).reshape(T, PEER_HEADS, PEER_TOPK * PEER_TOPK)
    sc, j = lax.top_k(cand, PEER_TOPK)
    eid = jnp.take_along_axis(cid, j, axis=-1)
    gate = jax.nn.softmax(sc, axis=-1)
    act = jax.nn.gelu(jnp.einsum('td,thkd->thk', x.astype(jnp.float32), u_tab[eid].astype(jnp.float32)), approximate=False)
    out = jnp.einsum('thk,thkd->td', gate * act, v_tab[eid].astype(jnp.float32))
    return out.astype(x.dtype)


def peer(x, w_q, sub_keys, u_tab, v_tab):
    shp = x.shape
    xf = x.reshape(-1, shp[-1])
    n = xf.shape[0]
    blk = min(PEER_TOK_BLOCK, n)
    pad = -(-n // blk) * blk - n
    xs = jnp.pad(xf, ((0, pad), (0, 0))).reshape(-1, blk, shp[-1])
    out = lax.map(lambda t: peer_block(t, w_q, sub_keys, u_tab, v_tab), xs)
    return out.reshape(-1, shp[-1])[:n].reshape(shp)


def _norm3(x, g):
    bsz, L, d = x.shape
    return _rmsnorm_bf16(x.reshape(bsz * L, d), g).reshape(bsz, L, d).astype(jnp.float32)


def kernel(x_prompt, x_sample, cache_k, cache_v, state_delta, state_conv, page_table,
           norm_mix, norm_ffn, ab_w_in, ab_w_out, a_conv_w, a_log, a_dt_bias, a_out_norm,
           b_q_norm, b_k_norm, c_w_in, c_b_in, c_ln_g, c_ln_b, c_w_s, c_b_s, c_w_out,
           peer_w_q, peer_sub_keys, peer_u, peer_v):
    bp = x_prompt.shape[0]
    bs = x_sample.shape[0]
    past_len = page_table.shape[1] * cache_k.shape[2]
    bf16 = jnp.bfloat16
    yp, ys = x_prompt, x_sample
    kp_l, vp_l, dp_l, cp_l = [], [], [], []
    ks_l, vs_l, ds_l, cs_l = [], [], [], []
    sv_l = []
    for layer in range(DEPTH):
        hp = _norm3(yp, norm_mix[layer])
        hs = _norm3(ys, norm_mix[layer])
        if layer % 2 == 0:
            i = layer // 2
            w_in = ab_w_in[i]
            split = A_CONV_DIM + A_V_W
            w_small = jnp.pad(w_in[:, split:P_A], ((0, 0), (0, 128 - 2 * A_HEADS))).astype(bf16)
            w_in_parts = (w_in[:, :split].astype(bf16), w_small, w_in[:, P_A:].astype(bf16))
            w = (w_in_parts, ab_w_out[i].astype(bf16), a_conv_w[i], a_log[i], a_dt_bias[i], a_out_norm[i],
                 b_q_norm[i], b_k_norm[i])
            k_past = cache_k[i, page_table].reshape(bs, past_len, B_HEADS, B_DH)
            v_past = cache_v[i, page_table].reshape(bs, past_len, B_HEADS, B_DH)
            conv0 = jnp.zeros((bp, A_CONV - 1, A_CONV_DIM), x_prompt.dtype)
            s0 = jnp.zeros((bp, A_HEADS, A_DK, A_DV), jnp.float32)
            mp, kp, vp, cp, dp = ab_mixer(hp, 0, conv0, s0, [], [], *w)
            ms, kn, vn, cn, dn = ab_mixer(hs, past_len, state_conv[i], state_delta[i], [k_past], [v_past], *w)
            kp_l.append(kp); vp_l.append(vp); dp_l.append(dp); cp_l.append(cp)
            ks_l.append(kn); vs_l.append(vn); ds_l.append(dn); cs_l.append(cn)
        else:
            j = layer // 2
            w = (c_w_in[j].astype(bf16), c_b_in[j], c_ln_g[j], c_ln_b[j], c_w_s[j], c_b_s[j], c_w_out[j].astype(bf16))
            mp, _ = c_mixer(hp, *w)
            ms, sv = c_mixer(hs, *w)
            sv_l.append(sv)
        yp = yp + mp
        ys = ys + ms
        pw = (peer_w_q[layer].astype(bf16), peer_sub_keys[layer], peer_u[layer], peer_v[layer])
        yp = yp + peer(_norm3(yp, norm_ffn[layer]), *pw)
        ys = ys + peer(_norm3(ys, norm_ffn[layer]), *pw)
    return (yp, ys, jnp.stack(kp_l), jnp.stack(vp_l), jnp.stack(dp_l), jnp.stack(cp_l),
            jnp.stack(ks_l), jnp.stack(vs_l), jnp.stack(ds_l), jnp.stack(cs_l), jnp.stack(sv_l))
```

```python
import functools
import math

import jax
import jax.numpy as jnp
from jax import lax
from jax.experimental import pallas as pl
from jax.experimental.pallas import tpu as pltpu

D_MODEL = 4096
DEPTH = 2

A_HEADS = 16
A_DK = 128
A_DV = 128
A_CONV = 4
A_CHUNK = 64
A_QK_W = A_HEADS * A_DK
A_V_W = A_HEADS * A_DV
A_CONV_DIM = 2 * A_QK_W + A_V_W
P_A = A_CONV_DIM + A_V_W + 2 * A_HEADS

B_HEADS = 16
B_DH = 128
B_W = B_HEADS * B_DH
MOBA_BLOCK = 256
MOBA_TOPK = 3
ROPE_THETA = 500000.0
ROPE_DIM = B_DH // 4
Q_BLOCK = 128

C_WIDTH = D_MODEL
C_GROUPS = 16
C_GROUP_W = C_WIDTH // C_GROUPS
C_CHUNK = 128

PEER_HEADS = 8
PEER_NKEYS = 128
PEER_QDIM = 256
PEER_TOPK = 16
PEER_TOK_BLOCK = 128

NORM_EPS = 1e-6
LN_EPS = 1e-5

V7X_VMEM_LIMIT_BYTES = 48 * 1024 * 1024


def _mm_kernel(a_ref, b_ref, o_ref):
    o_ref[...] = jnp.dot(a_ref[...], b_ref[...], preferred_element_type=jnp.float32)


def _mm_res_kernel(a_ref, b_ref, r_ref, o_ref):
    o_ref[...] = r_ref[...] + jnp.dot(a_ref[...], b_ref[...], preferred_element_type=jnp.float32)


def _matmul(a, b, residual=None):
    m, k = a.shape
    n = b.shape[1]
    tm = min(m, 512)
    tn = min(n, 1024)
    assert m % tm == 0 and n % tn == 0, (m, n)
    grid = (n // tn, m // tm)
    in_specs = [pl.BlockSpec((tm, k), lambda j, i: (i, 0)),
                pl.BlockSpec((k, tn), lambda j, i: (0, j))]
    args = [a, b]
    body = _mm_kernel
    if residual is not None:
        in_specs.append(pl.BlockSpec((tm, tn), lambda j, i: (i, j)))
        args.append(residual)
        body = _mm_res_kernel
    return pl.pallas_call(
        body,
        grid=grid,
        in_specs=in_specs,
        out_specs=pl.BlockSpec((tm, tn), lambda j, i: (i, j)),
        out_shape=jax.ShapeDtypeStruct((m, n), jnp.float32),
        compiler_params=pltpu.CompilerParams(
            dimension_semantics=("arbitrary", "arbitrary"),
            vmem_limit_bytes=V7X_VMEM_LIMIT_BYTES),
        name="matmul",
    )(*args)


def _rms_kernel(x_ref, g_ref, o_ref):
    x = x_ref[...]
    y = x * lax.rsqrt(jnp.mean(x * x, axis=-1, keepdims=True) + NORM_EPS)
    o_ref[...] = (y * g_ref[...]).astype(o_ref.dtype)


def _rmsnorm(x, g, out_dtype):
    t, d = x.shape
    tr = min(t, 256)
    assert t % tr == 0
    return pl.pallas_call(
        _rms_kernel,
        grid=(t // tr,),
        in_specs=[pl.BlockSpec((tr, d), lambda i: (i, 0)),
                  pl.BlockSpec((1, d), lambda i: (0, 0))],
        out_specs=pl.BlockSpec((tr, d), lambda i: (i, 0)),
        out_shape=jax.ShapeDtypeStruct((t, d), out_dtype),
        compiler_params=pltpu.CompilerParams(dimension_semantics=("arbitrary",)),
        name="rmsnorm",
    )(x, g.reshape(1, d))


def rms_norm(x, g):
    xf = x.astype(jnp.float32)
    y = xf * lax.rsqrt(jnp.mean(xf * xf, axis=-1, keepdims=True) + NORM_EPS)
    return (y * g.astype(jnp.float32)).astype(x.dtype)


def layer_norm(x, g, b):
    xf = x.astype(jnp.float32)
    mu = jnp.mean(xf, axis=-1, keepdims=True)
    xc = xf - mu
    y = xc * lax.rsqrt(jnp.mean(xc * xc, axis=-1, keepdims=True) + LN_EPS)
    return (y * g.astype(jnp.float32) + b.astype(jnp.float32)).astype(x.dtype)


def l2_norm(x):
    xf = x.astype(jnp.float32)
    return xf * lax.rsqrt(jnp.sum(xf * xf, axis=-1, keepdims=True) + NORM_EPS)


def partial_rope(x, pos):
    half = ROPE_DIM // 2
    inv_freq = ROPE_THETA ** (-jnp.arange(half, dtype=jnp.float32) / half)
    ang = pos.astype(jnp.float32)[:, None] * inv_freq[None, :]
    cos = jnp.cos(ang)[:, None, :]
    sin = jnp.sin(ang)[:, None, :]
    xr = x[..., :ROPE_DIM].astype(jnp.float32)
    x1, x2 = xr[..., :half], xr[..., half:]
    rot = jnp.concatenate([x1 * cos - x2 * sin, x2 * cos + x1 * sin], axis=-1)
    return jnp.concatenate([rot.astype(x.dtype), x[..., ROPE_DIM:]], axis=-1)


def gated_delta_chunked(q, k, v, g, beta, s0):
    bsz, L, H, dk = q.shape
    dv = v.shape[-1]
    c = min(A_CHUNK, L)
    n = -(-L // c)
    pad = n * c - L

    def prep(t):
        t = t.astype(jnp.float32)
        t = jnp.pad(t, [(0, 0), (0, pad)] + [(0, 0)] * (t.ndim - 2))
        t = t.reshape((bsz, n, c) + t.shape[2:])
        return jnp.moveaxis(t, (1, 2), (0, 3))

    qc = prep(q) * dk ** -0.5
    kc, vc, gc, bc = prep(k), prep(v), prep(g), prep(beta)
    G = jnp.cumsum(gc, axis=-1)
    tri = jnp.tril(jnp.ones((c, c), bool))
    strict_tri = jnp.tril(jnp.ones((c, c), bool), -1)
    diff = G[..., :, None] - G[..., None, :]
    decay = jnp.where(tri, jnp.exp(jnp.where(tri, diff, 0.0)), 0.0)
    kbeta = kc * bc[..., None]
    lower = jnp.where(strict_tri, jnp.einsum('nbhid,nbhjd->nbhij', kbeta, kc) * decay, 0.0)
    eye = jnp.eye(c, dtype=jnp.float32)
    t_inv = lax.linalg.triangular_solve(eye + lower, jnp.broadcast_to(eye, lower.shape),
                                        left_side=True, lower=True, unit_diagonal=True)
    u = t_inv @ (vc * bc[..., None])
    w = t_inv @ (kbeta * jnp.exp(G)[..., None])
    qk = jnp.where(tri, jnp.einsum('nbhid,nbhjd->nbhij', qc, kc) * decay, 0.0)

    def step(S, xs):
        qi, ki, ui, wi, Gi, qki = xs
        v_new = ui - wi @ S
        o = (qi * jnp.exp(Gi)[..., None]) @ S + qki @ v_new
        g_last = Gi[..., -1]
        S = S * jnp.exp(g_last)[..., None, None] + jnp.einsum(
            'bhcd,bhce->bhde', ki * jnp.exp(g_last[..., None] - Gi)[..., None], v_new)
        return S, o

    S, o = lax.scan(step, s0.astype(jnp.float32), (qc, kc, u, w, G, qk))
    o = jnp.moveaxis(o, (0, 3), (1, 2)).reshape(bsz, n * c, H, dv)[:, :L]
    return o, S.astype(s0.dtype)


def mixer_a(pa, conv_buf, s0, conv_w, a_log, dt_bias, out_norm):
    bsz, L, _ = pa.shape
    qkv = pa[..., :A_CONV_DIM]
    z = pa[..., A_CONV_DIM:A_CONV_DIM + A_V_W]
    b_logit = pa[..., A_CONV_DIM + A_V_W:A_CONV_DIM + A_V_W + A_HEADS]
    a_in = pa[..., A_CONV_DIM + A_V_W + A_HEADS:]
    xp = jnp.concatenate([conv_buf.astype(qkv.dtype), qkv], axis=1)
    conv = jax.nn.silu(sum(xp[:, j:j + L] * conv_w[:, j] for j in range(A_CONV)))
    new_buf = xp[:, L:]
    q = l2_norm(conv[..., :A_QK_W].reshape(bsz, L, A_HEADS, A_DK))
    k = l2_norm(conv[..., A_QK_W:2 * A_QK_W].reshape(bsz, L, A_HEADS, A_DK))
    v = conv[..., 2 * A_QK_W:].reshape(bsz, L, A_HEADS, A_DV)
    beta = jax.nn.sigmoid(b_logit.astype(jnp.float32))
    g = -jnp.exp(a_log.astype(jnp.float32)) * jax.nn.softplus(a_in.astype(jnp.float32) + dt_bias.astype(jnp.float32))
    o, s_new = gated_delta_chunked(q, k, v, g, beta, s0)
    o = rms_norm(o, out_norm) * jax.nn.silu(z.reshape(bsz, L, A_HEADS, A_DV).astype(jnp.float32))
    return o.reshape(bsz, L, A_V_W).astype(pa.dtype), new_buf, s_new


def moba_attend_block(q, q_pos, kb, vb, kmean):
    n_blocks = kb.shape[0]
    n_heads = q.shape[1]
    scale = q.shape[-1] ** -0.5
    qf = q.astype(jnp.float32)
    own = q_pos[0] // MOBA_BLOCK
    offs = jnp.arange(MOBA_BLOCK, dtype=jnp.int32)
    ko = lax.dynamic_index_in_dim(kb, own, 0, keepdims=False).astype(jnp.float32)
    vo = lax.dynamic_index_in_dim(vb, own, 0, keepdims=False).astype(jnp.float32)
    s = jnp.einsum('qhd,khd->qhk', qf, ko) * scale
    s = jnp.where((own * MOBA_BLOCK + offs)[None, None, :] <= q_pos[:, None, None], s, -jnp.inf)
    m = jnp.max(s, axis=-1)
    p = jnp.exp(s - m[..., None])
    l = jnp.sum(p, axis=-1)
    acc = jnp.einsum('qhk,khd->qhd', p, vo)
    gate = jnp.einsum('qhd,nhd->qhn', qf, kmean)
    gate = jnp.where(jnp.arange(n_blocks)[None, None, :] < own, gate, -jnp.inf)
    _, sel = lax.top_k(gate, min(MOBA_TOPK, n_blocks))
    hid = jnp.arange(n_heads)[None, :]
    for slot in range(sel.shape[-1]):
        blk = sel[..., slot]
        ks = kb[blk, :, hid, :].astype(jnp.float32)
        vs = vb[blk, :, hid, :].astype(jnp.float32)
        s = jnp.einsum('qhd,qhkd->qhk', qf, ks) * scale
        s = jnp.where((blk < own)[..., None], s, -jnp.inf)
        m_new = jnp.maximum(m, jnp.max(s, axis=-1))
        corr = jnp.exp(m - m_new)
        p = jnp.exp(s - m_new[..., None])
        l = l * corr + jnp.sum(p, axis=-1)
        acc = acc * corr[..., None] + jnp.einsum('qhk,qhkd->qhd', p, vs)
        m = m_new
    return (acc / l[..., None]).astype(q.dtype)


def moba_attention(q, pos0, k_parts, v_parts):
    bsz, L, H, dh = q.shape
    qb = min(Q_BLOCK, L)
    nq = -(-L // qb)
    lp = nq * qb
    T = sum(t.shape[1] for t in k_parts)
    n_blocks = -(-(pos0 + lp) // MOBA_BLOCK)
    pad = jnp.zeros((bsz, n_blocks * MOBA_BLOCK - T, H, dh), q.dtype)
    kb = jnp.concatenate(k_parts + [pad], axis=1).reshape(bsz, n_blocks, MOBA_BLOCK, H, dh)
    vb = jnp.concatenate(v_parts + [pad], axis=1).reshape(bsz, n_blocks, MOBA_BLOCK, H, dh)
    kmean = jnp.mean(kb, axis=2, dtype=jnp.float32)
    qp = jnp.pad(q, ((0, 0), (0, lp - L), (0, 0), (0, 0))).reshape(bsz * nq, qb, H, dh)
    pos = jnp.tile((pos0 + jnp.arange(lp, dtype=jnp.int32)).reshape(nq, qb), (bsz, 1))
    bidx = jnp.repeat(jnp.arange(bsz, dtype=jnp.int32), nq)

    def one(args):
        qq, pp, bi = args
        return moba_attend_block(qq, pp, kb[bi], vb[bi], kmean[bi])

    out = lax.map(one, (qp, pos, bidx))
    return out.reshape(bsz, lp, H, dh)[:, :L]


def _project(h, w_bf16):
    bsz, L, d = h.shape
    return _matmul(h.reshape(bsz * L, d).astype(jnp.bfloat16), w_bf16).reshape(bsz, L, -1)


def ab_mixer(h, pos0, conv_buf, s0, k_past, v_past, w_in, w_out, conv_w, a_log, dt_bias, out_norm, q_norm, k_norm):
    bsz, L, _ = h.shape
    w_main, w_small, w_b = w_in
    p_main = _project(h, w_main)
    p_small = _project(h, w_small)[..., :2 * A_HEADS]
    pb = _project(h, w_b)
    pa = jnp.concatenate([p_main, p_small], axis=-1)
    oa, conv_new, s_new = mixer_a(pa, conv_buf, s0, conv_w, a_log, dt_bias, out_norm)
    pos = pos0 + jnp.arange(L, dtype=jnp.int32)
    q = partial_rope(rms_norm(pb[..., :B_W].reshape(bsz, L, B_HEADS, B_DH), q_norm), pos)
    k = partial_rope(rms_norm(pb[..., B_W:2 * B_W].reshape(bsz, L, B_HEADS, B_DH), k_norm), pos)
    v = pb[..., 2 * B_W:].reshape(bsz, L, B_HEADS, B_DH)
    if k_past:
        ob = moba_attention(q, pos0, k_past + [k], v_past + [v]).reshape(bsz, L, B_W)
    else:
        ob = moba_prompt(q.reshape(bsz, L, B_W), k.reshape(bsz, L, B_W), v.reshape(bsz, L, B_W), B_HEADS)
    cat = jnp.concatenate([oa.astype(jnp.bfloat16), ob.astype(jnp.bfloat16)], axis=-1)
    y = _project(cat, w_out)
    return y, k, v, conv_new, s_new


def c_mixer(h, w_in, b_in, ln_g, ln_b, w_s, b_s, w_out):
    bsz, L, _ = h.shape
    z = jax.nn.gelu(_project(h, w_in) + b_in, approximate=False)
    u, v = z[..., :C_WIDTH], z[..., C_WIDTH:]
    v = layer_norm(v, ln_g, ln_b)
    c = min(C_CHUNK, L)
    n = -(-L // c)
    lp = n * c
    vp = jnp.pad(v, ((0, 0), (0, lp - L), (0, 0))).reshape(bsz, n, c, C_GROUPS, C_GROUP_W)
    ws = jnp.tril(w_s[:, :c, :c])
    mix = jnp.einsum('gij,bnjgc->bnigc', ws, vp) + b_s[:, :c].T[None, None, :, :, None]
    mix = mix.reshape(bsz, lp, C_WIDTH)[:, :L]
    return _project(u * mix, w_out), v


NEG_INF = float("-inf")


def _moba_prompt_kernel(q_ref, k_ref, v_ref, o_ref, kbf, vbf, kmean, *, n_blocks):
    i = pl.program_id(2)
    blk = MOBA_BLOCK

    @pl.when(i == 0)
    def _():
        k = k_ref[0]
        kbf[...] = k.astype(jnp.bfloat16)
        vbf[...] = v_ref[0].astype(jnp.bfloat16)
        kmean[...] = jnp.mean(k.reshape(n_blocks, blk, B_DH), axis=1)

    own = i
    q = q_ref[0]
    gate = lax.dot_general(q, kmean[...], (((1,), (1,)), ((), ())),
                           precision=lax.Precision.HIGHEST,
                           preferred_element_type=jnp.float32)
    lane = lax.broadcasted_iota(jnp.int32, gate.shape, 1)
    avail = lane < own
    sel = jnp.zeros(gate.shape, jnp.bool_)
    for _ in range(MOBA_TOPK):
        g = jnp.where(avail, gate, NEG_INF)
        mx = jnp.max(g, axis=1, keepdims=True)
        first = jnp.min(jnp.where(avail & (g == mx), lane, n_blocks), axis=1, keepdims=True)
        pick = lane == first
        sel = sel | pick
        avail = avail & jnp.logical_not(pick)
    sel_f = sel.astype(jnp.float32)

    qs = (q * (B_DH ** -0.5)).astype(jnp.bfloat16)
    nt = (((1,), (1,)), ((), ()))

    start = pl.multiple_of(own * blk, blk)
    s = lax.dot_general(qs, kbf[pl.ds(start, blk), :], nt, preferred_element_type=jnp.float32)
    row = lax.broadcasted_iota(jnp.int32, s.shape, 0)
    col = lax.broadcasted_iota(jnp.int32, s.shape, 1)
    s = jnp.where(col <= row, s, NEG_INF)
    m0 = jnp.max(s, axis=1, keepdims=True)
    p = jnp.exp(s - m0)
    l0 = jnp.sum(p, axis=1, keepdims=True)
    acc0 = jnp.dot(p.astype(jnp.bfloat16), vbf[pl.ds(start, blk), :], preferred_element_type=jnp.float32)

    def body(n, carry):
        m, l, acc = carry
        st = pl.multiple_of(n * blk, blk)
        s = lax.dot_general(qs, kbf[pl.ds(st, blk), :], nt, preferred_element_type=jnp.float32)
        sel_n = jnp.sum(jnp.where(lane == n, sel_f, 0.0), axis=1, keepdims=True) > 0.0
        s = jnp.where(sel_n, s, NEG_INF)
        m_new = jnp.maximum(m, jnp.max(s, axis=1, keepdims=True))
        corr = jnp.exp(m - m_new)
        p = jnp.exp(s - m_new)
        l = l * corr + jnp.sum(p, axis=1, keepdims=True)
        acc = acc * corr + jnp.dot(p.astype(jnp.bfloat16), vbf[pl.ds(st, blk), :],
                                   preferred_element_type=jnp.float32)
        return m_new, l, acc

    m, l, acc = lax.fori_loop(0, own, body, (m0, l0, acc0))
    o_ref[0] = (acc / l).astype(o_ref.dtype)


def moba_prompt(q, k, v, n_heads):
    bsz, L, _ = q.shape
    assert L % MOBA_BLOCK == 0
    n_blocks = L // MOBA_BLOCK
    kern = functools.partial(_moba_prompt_kernel, n_blocks=n_blocks)
    return pl.pallas_call(
        kern,
        grid=(bsz, n_heads, n_blocks),
        in_specs=[pl.BlockSpec((1, MOBA_BLOCK, B_DH), lambda b, h, i: (b, i, h)),
                  pl.BlockSpec((1, L, B_DH), lambda b, h, i: (b, 0, h)),
                  pl.BlockSpec((1, L, B_DH), lambda b, h, i: (b, 0, h))],
        out_specs=pl.BlockSpec((1, MOBA_BLOCK, B_DH), lambda b, h, i: (b, i, h)),
        out_shape=jax.ShapeDtypeStruct(q.shape, jnp.bfloat16),
        scratch_shapes=[pltpu.VMEM((L, B_DH), jnp.bfloat16),
                        pltpu.VMEM((L, B_DH), jnp.bfloat16),
                        pltpu.VMEM((n_blocks, B_DH), jnp.float32)],
        compiler_params=pltpu.CompilerParams(
            dimension_semantics=("arbitrary", "arbitrary", "arbitrary")),
        name="moba_prompt",
    )(q, k, v)


N_SEL = PEER_HEADS * PEER_TOPK
SUB = 8
LANE = 128
N_CHUNK = N_SEL // SUB
INV_SQRT2 = 0.7071067811865476


def _gelu_exact(x):
    return 0.5 * x * (1.0 + lax.erf(x * INV_SQRT2))


def _peer_gather_kernel(eid_ref, res_ref, x_ref, g_ref, u_hbm, v_hbm, o_ref,
                        ubuf, vbuf, wbuf, pbuf, cbuf, sem, *, tb, d):
    n_tiles = d // LANE

    def start_rows(tab_hbm, buf, which, tok, slot, j):
        for r in range(SUB):
            k = j * SUB + r
            idx = eid_ref[tok, k]
            pltpu.make_async_copy(tab_hbm.at[idx], buf.at[slot, :, k, :],
                                  sem.at[which, slot]).start()

    def wait_rows(tab_hbm, which, slot):
        pltpu.make_async_copy(tab_hbm.at[pl.ds(0, N_SEL)], wbuf,
                              sem.at[which, slot]).wait()

    for j in range(N_CHUNK):
        start_rows(u_hbm, ubuf, 0, 0, 0, j)
        start_rows(v_hbm, vbuf, 1, 0, 0, j)

    eye = (lax.broadcasted_iota(jnp.int32, (N_SEL, LANE), 0)
           == lax.broadcasted_iota(jnp.int32, (N_SEL, LANE), 1))

    def token(t, c):
        slot = t % 2
        nslot = 1 - slot
        nxt = jnp.minimum(t + 1, tb - 1)
        x8 = jnp.broadcast_to(x_ref[pl.ds(t, 1), :], (SUB, d))

        wait_rows(u_hbm, 0, slot)

        for j in range(N_CHUNK):
            start_rows(u_hbm, ubuf, 0, nxt, nslot, j)
            row0 = j * SUB
            part = ubuf[slot, 0, pl.ds(row0, SUB), :] * x8[:, 0:LANE]
            for ct in range(1, n_tiles):
                part = part + ubuf[slot, ct, pl.ds(row0, SUB), :] * x8[:, ct * LANE:(ct + 1) * LANE]
            pbuf[pl.ds(row0, SUB), :] = part

        act = _gelu_exact(jnp.sum(pbuf[...], axis=1, keepdims=True))
        g_row = jnp.broadcast_to(g_ref[pl.ds(t, 1), :], (N_SEL, LANE))
        g_col = jnp.sum(jnp.where(eye, g_row, 0.0), axis=1, keepdims=True)
        cbuf[...] = jnp.broadcast_to(g_col * act, (N_SEL, LANE))

        wait_rows(v_hbm, 1, slot)

        acc = [jnp.zeros((SUB, LANE), jnp.float32)] * n_tiles
        for j in range(N_CHUNK):
            start_rows(v_hbm, vbuf, 1, nxt, nslot, j)
            row0 = j * SUB
            cc = cbuf[pl.ds(row0, SUB), :]
            acc = [acc[ct] + vbuf[slot, ct, pl.ds(row0, SUB), :] * cc for ct in range(n_tiles)]
        o_ref[pl.ds(t, 1), :] = res_ref[pl.ds(t, 1), :] + jnp.sum(jnp.concatenate(acc, axis=1), axis=0, keepdims=True)
        return c
    lax.fori_loop(0, tb, token, 0)

    last_slot = tb % 2
    wait_rows(u_hbm, 0, last_slot)
    wait_rows(v_hbm, 1, last_slot)


def peer_gather(res, xn, eid, gate, u_tab, v_tab, tb):
    t, d = xn.shape
    assert t % tb == 0 and d % LANE == 0
    kern = functools.partial(_peer_gather_kernel, tb=tb, d=d)
    return pl.pallas_call(
        kern,
        grid=(t // tb,),
        in_specs=[pl.BlockSpec((tb, N_SEL), lambda i: (i, 0), memory_space=pltpu.SMEM),
                  pl.BlockSpec((tb, d), lambda i: (i, 0)),
                  pl.BlockSpec((tb, d), lambda i: (i, 0)),
                  pl.BlockSpec((tb, N_SEL), lambda i: (i, 0)),
                  pl.BlockSpec(memory_space=pl.ANY),
                  pl.BlockSpec(memory_space=pl.ANY)],
        out_specs=pl.BlockSpec((tb, d), lambda i: (i, 0)),
        out_shape=jax.ShapeDtypeStruct((t, d), jnp.float32),
        scratch_shapes=[pltpu.VMEM((2, d // LANE, N_SEL, LANE), jnp.float32),
                        pltpu.VMEM((2, d // LANE, N_SEL, LANE), jnp.float32),
                        pltpu.VMEM((N_SEL, d // LANE, LANE), jnp.float32),
                        pltpu.VMEM((N_SEL, LANE), jnp.float32),
                        pltpu.VMEM((N_SEL, LANE), jnp.float32),
                        pltpu.SemaphoreType.DMA((2, 2))],
        compiler_params=pltpu.CompilerParams(
            dimension_semantics=("arbitrary",),
            vmem_limit_bytes=V7X_VMEM_LIMIT_BYTES),
        name="peer_gather",
    )(eid, res, xn, gate, u_tab, v_tab)


def peer_route(q, sub_keys):
    t = q.shape[0]
    q = q.reshape(t, PEER_HEADS, 2, PEER_QDIM // 2)
    s = jnp.einsum('thpd,pkd->thpk', q, sub_keys, precision=lax.Precision.HIGHEST)
    s1, i1 = lax.top_k(s[:, :, 0], PEER_TOPK)
    s2, i2 = lax.top_k(s[:, :, 1], PEER_TOPK)
    cand = (s1[..., :, None] + s2[..., None, :]).reshape(t, PEER_HEADS, PEER_TOPK * PEER_TOPK)
    cid = (i1[..., :, None] * PEER_NKEYS + i2[..., None, :]).reshape(t, PEER_HEADS, PEER_TOPK * PEER_TOPK)
    sc, j = lax.top_k(cand, PEER_TOPK)
    eid = jnp.take_along_axis(cid, j, axis=-1)
    gate = jax.nn.softmax(sc, axis=-1)
    return eid.reshape(t, N_SEL).astype(jnp.int32), gate.reshape(t, N_SEL)


def peer_layer(y, g, w_q, sub_keys, u_tab, v_tab):
    bsz, L, d = y.shape
    yf = y.reshape(bsz * L, d)
    xn = _rmsnorm(yf, g, jnp.float32)
    q = _matmul(xn.astype(jnp.bfloat16), w_q)
    eid, gate = peer_route(q, sub_keys)
    out = peer_gather(yf, xn, eid, gate, u_tab, v_tab, min(PEER_TOK_BLOCK, bsz * L))
    return out.reshape(bsz, L, d)


def _norm3(x, g):
    bsz, L, d = x.shape
    return _rmsnorm(x.reshape(bsz * L, d), g, jnp.bfloat16).reshape(bsz, L, d)


def kernel(x_prompt, x_sample, cache_k, cache_v, state_delta, state_conv, page_table,
           norm_mix, norm_ffn, ab_w_in, ab_w_out, a_conv_w, a_log, a_dt_bias, a_out_norm,
           b_q_norm, b_k_norm, c_w_in, c_b_in, c_ln_g, c_ln_b, c_w_s, c_b_s, c_w_out,
           peer_w_q, peer_sub_keys, peer_u, peer_v):
    bp = x_prompt.shape[0]
    bs = x_sample.shape[0]
    past_len = page_table.shape[1] * cache_k.shape[2]
    bf16 = jnp.bfloat16
    yp, ys = x_prompt, x_sample
    kp_l, vp_l, dp_l, cp_l = [], [], [], []
    ks_l, vs_l, ds_l, cs_l = [], [], [], []
    sv_l = []
    for layer in range(DEPTH):
        hp = _norm3(yp, norm_mix[layer])
        hs = _norm3(ys, norm_mix[layer])
        if layer % 2 == 0:
            i = layer // 2
            w_in = ab_w_in[i]
            split = A_CONV_DIM + A_V_W
            w_small = jnp.pad(w_in[:, split:P_A], ((0, 0), (0, 128 - 2 * A_HEADS))).astype(bf16)
            w_in_parts = (w_in[:, :split].astype(bf16), w_small, w_in[:, P_A:].astype(bf16))
            w = (w_in_parts, ab_w_out[i].astype(bf16), a_conv_w[i], a_log[i], a_dt_bias[i], a_out_norm[i],
                 b_q_norm[i], b_k_norm[i])
            k_past = cache_k[i, page_table].reshape(bs, past_len, B_HEADS, B_DH)
            v_past = cache_v[i, page_table].reshape(bs, past_len, B_HEADS, B_DH)
            conv0 = jnp.zeros((bp, A_CONV - 1, A_CONV_DIM), x_prompt.dtype)
            s0 = jnp.zeros((bp, A_HEADS, A_DK, A_DV), jnp.float32)
            mp, kp, vp, cp, dp = ab_mixer(hp, 0, conv0, s0, [], [], *w)
            ms, kn, vn, cn, dn = ab_mixer(hs, past_len, state_conv[i], state_delta[i], [k_past], [v_past], *w)
            kp_l.append(kp); vp_l.append(vp); dp_l.append(dp); cp_l.append(cp)
            ks_l.append(kn); vs_l.append(vn); ds_l.append(dn); cs_l.append(cn)
        else:
            j = layer // 2
            w = (c_w_in[j].astype(bf16), c_b_in[j], c_ln_g[j], c_ln_b[j], c_w_s[j], c_b_s[j], c_w_out[j].astype(bf16))
            mp, _ = c_mixer(hp, *w)
            ms, sv = c_mixer(hs, *w)
            sv_l.append(sv)
        yp = yp + mp
        ys = ys + ms
        d3 = (-1, D_MODEL // LANE, LANE)
        pw = (peer_w_q[layer].astype(bf16), peer_sub_keys[layer],
              peer_u[layer].reshape(d3), peer_v[layer].reshape(d3))
        yp = peer_layer(yp, norm_ffn[layer], *pw)
        ys = peer_layer(ys, norm_ffn[layer], *pw)
    return (yp, ys, jnp.stack(kp_l), jnp.stack(vp_l), jnp.stack(dp_l), jnp.stack(cp_l),
            jnp.stack(ks_l), jnp.stack(vs_l), jnp.stack(ds_l), jnp.stack(cs_l), jnp.stack(sv_l))
```

```python
import functools
import math

import jax
import jax.numpy as jnp
from jax import lax
from jax.experimental import pallas as pl
from jax.experimental.pallas import tpu as pltpu

D_MODEL = 4096
DEPTH = 2

A_HEADS = 16
A_DK = 128
A_DV = 128
A_CONV = 4
A_CHUNK = 64
A_QK_W = A_HEADS * A_DK
A_V_W = A_HEADS * A_DV
A_CONV_DIM = 2 * A_QK_W + A_V_W
P_A = A_CONV_DIM + A_V_W + 2 * A_HEADS

B_HEADS = 16
B_DH = 128
B_W = B_HEADS * B_DH
MOBA_BLOCK = 256
MOBA_TOPK = 3
ROPE_THETA = 500000.0
ROPE_DIM = B_DH // 4
Q_BLOCK = 128

C_WIDTH = D_MODEL
C_GROUPS = 16
C_GROUP_W = C_WIDTH // C_GROUPS
C_CHUNK = 128

PEER_HEADS = 8
PEER_NKEYS = 128
PEER_QDIM = 256
PEER_TOPK = 16
PEER_TOK_BLOCK = 128

NORM_EPS = 1e-6
LN_EPS = 1e-5

V7X_VMEM_LIMIT_BYTES = 48 * 1024 * 1024


def _mm_kernel(a_ref, b_ref, o_ref):
    o_ref[...] = jnp.dot(a_ref[...], b_ref[...], preferred_element_type=jnp.float32)


def _mm_res_kernel(a_ref, b_ref, r_ref, o_ref):
    o_ref[...] = r_ref[...] + jnp.dot(a_ref[...], b_ref[...], preferred_element_type=jnp.float32)


def _matmul(a, b, residual=None):
    m, k = a.shape
    n = b.shape[1]
    tm = min(m, 512)
    tn = min(n, 1024)
    assert m % tm == 0 and n % tn == 0, (m, n)
    grid = (n // tn, m // tm)
    in_specs = [pl.BlockSpec((tm, k), lambda j, i: (i, 0)),
                pl.BlockSpec((k, tn), lambda j, i: (0, j))]
    args = [a, b]
    body = _mm_kernel
    if residual is not None:
        in_specs.append(pl.BlockSpec((tm, tn), lambda j, i: (i, j)))
        args.append(residual)
        body = _mm_res_kernel
    return pl.pallas_call(
        body,
        grid=grid,
        in_specs=in_specs,
        out_specs=pl.BlockSpec((tm, tn), lambda j, i: (i, j)),
        out_shape=jax.ShapeDtypeStruct((m, n), jnp.float32),
        compiler_params=pltpu.CompilerParams(
            dimension_semantics=("arbitrary", "arbitrary"),
            vmem_limit_bytes=V7X_VMEM_LIMIT_BYTES),
        name="matmul",
    )(*args)


def _rms_kernel(x_ref, g_ref, o_ref):
    x = x_ref[...]
    y = x * lax.rsqrt(jnp.mean(x * x, axis=-1, keepdims=True) + NORM_EPS)
    o_ref[...] = (y * g_ref[...]).astype(o_ref.dtype)


def _rmsnorm(x, g, out_dtype):
    t, d = x.shape
    tr = min(t, 256)
    assert t % tr == 0
    return pl.pallas_call(
        _rms_kernel,
        grid=(t // tr,),
        in_specs=[pl.BlockSpec((tr, d), lambda i: (i, 0)),
                  pl.BlockSpec((1, d), lambda i: (0, 0))],
        out_specs=pl.BlockSpec((tr, d), lambda i: (i, 0)),
        out_shape=jax.ShapeDtypeStruct((t, d), out_dtype),
        compiler_params=pltpu.CompilerParams(dimension_semantics=("arbitrary",)),
        name="rmsnorm",
    )(x, g.reshape(1, d))


def rms_norm(x, g):
    xf = x.astype(jnp.float32)
    y = xf * lax.rsqrt(jnp.mean(xf * xf, axis=-1, keepdims=True) + NORM_EPS)
    return (y * g.astype(jnp.float32)).astype(x.dtype)


def layer_norm(x, g, b):
    xf = x.astype(jnp.float32)
    mu = jnp.mean(xf, axis=-1, keepdims=True)
    xc = xf - mu
    y = xc * lax.rsqrt(jnp.mean(xc * xc, axis=-1, keepdims=True) + LN_EPS)
    return (y * g.astype(jnp.float32) + b.astype(jnp.float32)).astype(x.dtype)


def l2_norm(x):
    xf = x.astype(jnp.float32)
    return xf * lax.rsqrt(jnp.sum(xf * xf, axis=-1, keepdims=True) + NORM_EPS)


def partial_rope(x, pos):
    half = ROPE_DIM // 2
    inv_freq = ROPE_THETA ** (-jnp.arange(half, dtype=jnp.float32) / half)
    ang = pos.astype(jnp.float32)[:, None] * inv_freq[None, :]
    cos = jnp.cos(ang)[:, None, :]
    sin = jnp.sin(ang)[:, None, :]
    xr = x[..., :ROPE_DIM].astype(jnp.float32)
    x1, x2 = xr[..., :half], xr[..., half:]
    rot = jnp.concatenate([x1 * cos - x2 * sin, x2 * cos + x1 * sin], axis=-1)
    return jnp.concatenate([rot.astype(x.dtype), x[..., ROPE_DIM:]], axis=-1)


def gated_delta_chunked(q, k, v, g, beta, s0):
    bsz, L, H, dk = q.shape
    dv = v.shape[-1]
    c = min(A_CHUNK, L)
    n = -(-L // c)
    pad = n * c - L

    def prep(t):
        t = t.astype(jnp.float32)
        t = jnp.pad(t, [(0, 0), (0, pad)] + [(0, 0)] * (t.ndim - 2))
        t = t.reshape((bsz, n, c) + t.shape[2:])
        return jnp.moveaxis(t, (1, 2), (0, 3))

    qc = prep(q) * dk ** -0.5
    kc, vc, gc, bc = prep(k), prep(v), prep(g), prep(beta)
    G = jnp.cumsum(gc, axis=-1)
    tri = jnp.tril(jnp.ones((c, c), bool))
    strict_tri = jnp.tril(jnp.ones((c, c), bool), -1)
    diff = G[..., :, None] - G[..., None, :]
    decay = jnp.where(tri, jnp.exp(jnp.where(tri, diff, 0.0)), 0.0)
    kbeta = kc * bc[..., None]
    lower = jnp.where(strict_tri, jnp.einsum('nbhid,nbhjd->nbhij', kbeta, kc) * decay, 0.0)
    hi = lax.Precision.HIGHEST
    t_inv = jnp.eye(c, dtype=jnp.float32) - lower
    pw2 = lower
    for _ in range(max(c - 1, 0).bit_length() - 1):
        pw2 = jnp.matmul(pw2, pw2, precision=hi)
        t_inv = t_inv + jnp.matmul(t_inv, pw2, precision=hi)
    u = t_inv @ (vc * bc[..., None])
    w = t_inv @ (kbeta * jnp.exp(G)[..., None])
    qk = jnp.where(tri, jnp.einsum('nbhid,nbhjd->nbhij', qc, kc) * decay, 0.0)

    def step(S, xs):
        qi, ki, ui, wi, Gi, qki = xs
        v_new = ui - wi @ S
        o = (qi * jnp.exp(Gi)[..., None]) @ S + qki @ v_new
        g_last = Gi[..., -1]
        S = S * jnp.exp(g_last)[..., None, None] + jnp.einsum(
            'bhcd,bhce->bhde', ki * jnp.exp(g_last[..., None] - Gi)[..., None], v_new)
        return S, o

    S, o = lax.scan(step, s0.astype(jnp.float32), (qc, kc, u, w, G, qk))
    o = jnp.moveaxis(o, (0, 3), (1, 2)).reshape(bsz, n * c, H, dv)[:, :L]
    return o, S.astype(s0.dtype)


def mixer_a(pa, conv_buf, s0, conv_w, a_log, dt_bias, out_norm):
    bsz, L, _ = pa.shape
    qkv = pa[..., :A_CONV_DIM]
    z = pa[..., A_CONV_DIM:A_CONV_DIM + A_V_W]
    b_logit = pa[..., A_CONV_DIM + A_V_W:A_CONV_DIM + A_V_W + A_HEADS]
    a_in = pa[..., A_CONV_DIM + A_V_W + A_HEADS:]
    xp = jnp.concatenate([conv_buf.astype(qkv.dtype), qkv], axis=1)
    conv = jax.nn.silu(sum(xp[:, j:j + L] * conv_w[:, j] for j in range(A_CONV)))
    new_buf = xp[:, L:]
    q = l2_norm(conv[..., :A_QK_W].reshape(bsz, L, A_HEADS, A_DK))
    k = l2_norm(conv[..., A_QK_W:2 * A_QK_W].reshape(bsz, L, A_HEADS, A_DK))
    v = conv[..., 2 * A_QK_W:].reshape(bsz, L, A_HEADS, A_DV)
    beta = jax.nn.sigmoid(b_logit.astype(jnp.float32))
    g = -jnp.exp(a_log.astype(jnp.float32)) * jax.nn.softplus(a_in.astype(jnp.float32) + dt_bias.astype(jnp.float32))
    o, s_new = gated_delta_chunked(q, k, v, g, beta, s0)
    o = rms_norm(o, out_norm) * jax.nn.silu(z.reshape(bsz, L, A_HEADS, A_DV).astype(jnp.float32))
    return o.reshape(bsz, L, A_V_W).astype(pa.dtype), new_buf, s_new


def moba_attend_block(q, q_pos, kb, vb, kmean):
    n_blocks = kb.shape[0]
    n_heads = q.shape[1]
    scale = q.shape[-1] ** -0.5
    qf = q.astype(jnp.float32)
    own = q_pos[0] // MOBA_BLOCK
    offs = jnp.arange(MOBA_BLOCK, dtype=jnp.int32)
    ko = lax.dynamic_index_in_dim(kb, own, 0, keepdims=False).astype(jnp.float32)
    vo = lax.dynamic_index_in_dim(vb, own, 0, keepdims=False).astype(jnp.float32)
    s = jnp.einsum('qhd,khd->qhk', qf, ko) * scale
    s = jnp.where((own * MOBA_BLOCK + offs)[None, None, :] <= q_pos[:, None, None], s, -jnp.inf)
    m = jnp.max(s, axis=-1)
    p = jnp.exp(s - m[..., None])
    l = jnp.sum(p, axis=-1)
    acc = jnp.einsum('qhk,khd->qhd', p, vo)
    gate = jnp.einsum('qhd,nhd->qhn', qf, kmean)
    gate = jnp.where(jnp.arange(n_blocks)[None, None, :] < own, gate, -jnp.inf)
    _, sel = lax.top_k(gate, min(MOBA_TOPK, n_blocks))
    hid = jnp.arange(n_heads)[None, :]
    for slot in range(sel.shape[-1]):
        blk = sel[..., slot]
        ks = kb[blk, :, hid, :].astype(jnp.float32)
        vs = vb[blk, :, hid, :].astype(jnp.float32)
        s = jnp.einsum('qhd,qhkd->qhk', qf, ks) * scale
        s = jnp.where((blk < own)[..., None], s, -jnp.inf)
        m_new = jnp.maximum(m, jnp.max(s, axis=-1))
        corr = jnp.exp(m - m_new)
        p = jnp.exp(s - m_new[..., None])
        l = l * corr + jnp.sum(p, axis=-1)
        acc = acc * corr[..., None] + jnp.einsum('qhk,qhkd->qhd', p, vs)
        m = m_new
    return (acc / l[..., None]).astype(q.dtype)


def moba_attention(q, pos0, k_parts, v_parts):
    bsz, L, H, dh = q.shape
    qb = min(Q_BLOCK, L)
    nq = -(-L // qb)
    lp = nq * qb
    T = sum(t.shape[1] for t in k_parts)
    n_blocks = -(-(pos0 + lp) // MOBA_BLOCK)
    pad = jnp.zeros((bsz, n_blocks * MOBA_BLOCK - T, H, dh), q.dtype)
    kb = jnp.concatenate(k_parts + [pad], axis=1).reshape(bsz, n_blocks, MOBA_BLOCK, H, dh)
    vb = jnp.concatenate(v_parts + [pad], axis=1).reshape(bsz, n_blocks, MOBA_BLOCK, H, dh)
    kmean = jnp.mean(kb, axis=2, dtype=jnp.float32)
    qp = jnp.pad(q, ((0, 0), (0, lp - L), (0, 0), (0, 0))).reshape(bsz * nq, qb, H, dh)
    pos = jnp.tile((pos0 + jnp.arange(lp, dtype=jnp.int32)).reshape(nq, qb), (bsz, 1))
    bidx = jnp.repeat(jnp.arange(bsz, dtype=jnp.int32), nq)

    def one(args):
        qq, pp, bi = args
        return moba_attend_block(qq, pp, kb[bi], vb[bi], kmean[bi])

    out = lax.map(one, (qp, pos, bidx))
    return out.reshape(bsz, lp, H, dh)[:, :L]


def _project(h, w_bf16):
    bsz, L, d = h.shape
    return _matmul(h.reshape(bsz * L, d).astype(jnp.bfloat16), w_bf16).reshape(bsz, L, -1)


def ab_mixer(h, pos0, conv_buf, s0, k_past, v_past, w_in, w_out, conv_w, a_log, dt_bias, out_norm, q_norm, k_norm):
    bsz, L, _ = h.shape
    w_main, w_small, w_b = w_in
    p_main = _project(h, w_main)
    p_small = _project(h, w_small)[..., :2 * A_HEADS]
    pb = _project(h, w_b)
    pa = jnp.concatenate([p_main, p_small], axis=-1)
    oa, conv_new, s_new = mixer_a(pa, conv_buf, s0, conv_w, a_log, dt_bias, out_norm)
    pos = pos0 + jnp.arange(L, dtype=jnp.int32)
    q = partial_rope(rms_norm(pb[..., :B_W].reshape(bsz, L, B_HEADS, B_DH), q_norm), pos)
    k = partial_rope(rms_norm(pb[..., B_W:2 * B_W].reshape(bsz, L, B_HEADS, B_DH), k_norm), pos)
    v = pb[..., 2 * B_W:].reshape(bsz, L, B_HEADS, B_DH)
    if k_past:
        ob = moba_attention(q, pos0, k_past + [k], v_past + [v]).reshape(bsz, L, B_W)
    else:
        ob = moba_prompt(q.reshape(bsz, L, B_W), k.reshape(bsz, L, B_W), v.reshape(bsz, L, B_W), B_HEADS)
    cat = jnp.concatenate([oa.astype(jnp.bfloat16), ob.astype(jnp.bfloat16)], axis=-1)
    y = _project(cat, w_out)
    return y, k, v, conv_new, s_new


def c_mixer(h, w_in, b_in, ln_g, ln_b, w_s, b_s, w_out):
    bsz, L, _ = h.shape
    z = jax.nn.gelu(_project(h, w_in) + b_in, approximate=False)
    u, v = z[..., :C_WIDTH], z[..., C_WIDTH:]
    v = layer_norm(v, ln_g, ln_b)
    c = min(C_CHUNK, L)
    n = -(-L // c)
    lp = n * c
    vp = jnp.pad(v, ((0, 0), (0, lp - L), (0, 0))).reshape(bsz, n, c, C_GROUPS, C_GROUP_W)
    ws = jnp.tril(w_s[:, :c, :c])
    mix = jnp.einsum('gij,bnjgc->bnigc', ws, vp) + b_s[:, :c].T[None, None, :, :, None]
    mix = mix.reshape(bsz, lp, C_WIDTH)[:, :L]
    return _project(u * mix, w_out), v


NEG_INF = float("-inf")


def _moba_prompt_kernel(q_ref, k_ref, v_ref, o_ref, kbf, vbf, kmean, *, n_blocks):
    i = pl.program_id(2)
    blk = MOBA_BLOCK

    @pl.when(i == 0)
    def _():
        k = k_ref[0]
        kbf[...] = k.astype(jnp.bfloat16)
        vbf[...] = v_ref[0].astype(jnp.bfloat16)
        kmean[...] = jnp.mean(k.reshape(n_blocks, blk, B_DH), axis=1)

    own = i
    q = q_ref[0]
    gate = lax.dot_general(q, kmean[...], (((1,), (1,)), ((), ())),
                           precision=lax.Precision.HIGHEST,
                           preferred_element_type=jnp.float32)
    lane = lax.broadcasted_iota(jnp.int32, gate.shape, 1)
    avail = lane < own
    sel = jnp.zeros(gate.shape, jnp.bool_)
    for _ in range(MOBA_TOPK):
        g = jnp.where(avail, gate, NEG_INF)
        mx = jnp.max(g, axis=1, keepdims=True)
        first = jnp.min(jnp.where(avail & (g == mx), lane, n_blocks), axis=1, keepdims=True)
        pick = lane == first
        sel = sel | pick
        avail = avail & jnp.logical_not(pick)
    sel_f = sel.astype(jnp.float32)

    qs = (q * (B_DH ** -0.5)).astype(jnp.bfloat16)
    nt = (((1,), (1,)), ((), ()))

    start = pl.multiple_of(own * blk, blk)
    s = lax.dot_general(qs, kbf[pl.ds(start, blk), :], nt, preferred_element_type=jnp.float32)
    row = lax.broadcasted_iota(jnp.int32, s.shape, 0)
    col = lax.broadcasted_iota(jnp.int32, s.shape, 1)
    s = jnp.where(col <= row, s, NEG_INF)
    m0 = jnp.max(s, axis=1, keepdims=True)
    p = jnp.exp(s - m0)
    l0 = jnp.sum(p, axis=1, keepdims=True)
    acc0 = jnp.dot(p.astype(jnp.bfloat16), vbf[pl.ds(start, blk), :], preferred_element_type=jnp.float32)

    def body(n, carry):
        m, l, acc = carry
        st = pl.multiple_of(n * blk, blk)
        s = lax.dot_general(qs, kbf[pl.ds(st, blk), :], nt, preferred_element_type=jnp.float32)
        sel_n = jnp.sum(jnp.where(lane == n, sel_f, 0.0), axis=1, keepdims=True) > 0.0
        s = jnp.where(sel_n, s, NEG_INF)
        m_new = jnp.maximum(m, jnp.max(s, axis=1, keepdims=True))
        corr = jnp.exp(m - m_new)
        p = jnp.exp(s - m_new)
        l = l * corr + jnp.sum(p, axis=1, keepdims=True)
        acc = acc * corr + jnp.dot(p.astype(jnp.bfloat16), vbf[pl.ds(st, blk), :],
                                   preferred_element_type=jnp.float32)
        return m_new, l, acc

    m, l, acc = lax.fori_loop(0, own, body, (m0, l0, acc0))
    o_ref[0] = (acc / l).astype(o_ref.dtype)


def moba_prompt(q, k, v, n_heads):
    bsz, L, _ = q.shape
    assert L % MOBA_BLOCK == 0
    n_blocks = L // MOBA_BLOCK
    kern = functools.partial(_moba_prompt_kernel, n_blocks=n_blocks)
    return pl.pallas_call(
        kern,
        grid=(bsz, n_heads, n_blocks),
        in_specs=[pl.BlockSpec((1, MOBA_BLOCK, B_DH), lambda b, h, i: (b, i, h)),
                  pl.BlockSpec((1, L, B_DH), lambda b, h, i: (b, 0, h)),
                  pl.BlockSpec((1, L, B_DH), lambda b, h, i: (b, 0, h))],
        out_specs=pl.BlockSpec((1, MOBA_BLOCK, B_DH), lambda b, h, i: (b, i, h)),
        out_shape=jax.ShapeDtypeStruct(q.shape, jnp.bfloat16),
        scratch_shapes=[pltpu.VMEM((L, B_DH), jnp.bfloat16),
                        pltpu.VMEM((L, B_DH), jnp.bfloat16),
                        pltpu.VMEM((n_blocks, B_DH), jnp.float32)],
        compiler_params=pltpu.CompilerParams(
            dimension_semantics=("arbitrary", "arbitrary", "arbitrary")),
        name="moba_prompt",
    )(q, k, v)


N_SEL = PEER_HEADS * PEER_TOPK
SUB = 8
LANE = 128
N_CHUNK = N_SEL // SUB
INV_SQRT2 = 0.7071067811865476


def _gelu_exact(x):
    return 0.5 * x * (1.0 + lax.erf(x * INV_SQRT2))


def _unpack(w):
    lo = lax.bitcast_convert_type(w << 16, jnp.float32)
    hi = lax.bitcast_convert_type(w & jnp.uint32(0xFFFF0000), jnp.float32)
    return lo, hi


def _peer_gather_kernel(eid_ref, res_ref, x_ref, g_ref, tab_hbm, o_ref,
                        buf, wbuf, pbuf, cbuf, sem, *, tb, d):
    n_half = d // LANE // 2
    per_step = N_SEL // (2 * N_CHUNK)

    def start_rows(tok, slot, step):
        for r in range(per_step):
            k = step * per_step + r
            idx = eid_ref[tok, k]
            pltpu.make_async_copy(tab_hbm.at[idx], buf.at[slot, :, k, :], sem.at[slot]).start()

    def wait_rows(slot):
        pltpu.make_async_copy(tab_hbm.at[pl.ds(0, N_SEL)], wbuf, sem.at[slot]).wait()

    for step in range(2 * N_CHUNK):
        start_rows(0, 0, step)

    eye = (lax.broadcasted_iota(jnp.int32, (N_SEL, LANE), 0)
           == lax.broadcasted_iota(jnp.int32, (N_SEL, LANE), 1))

    def token(t, c):
        slot = t % 2
        nslot = 1 - slot
        nxt = jnp.minimum(t + 1, tb - 1)
        x8 = jnp.broadcast_to(x_ref[pl.ds(t, 1), :], (SUB, d))

        wait_rows(slot)

        for j in range(N_CHUNK):
            start_rows(nxt, nslot, j)
            row0 = j * SUB
            part = None
            for ct in range(n_half):
                lo, hi = _unpack(buf[slot, ct, pl.ds(row0, SUB), :])
                term = (lo * x8[:, ct * LANE:(ct + 1) * LANE]
                        + hi * x8[:, (n_half + ct) * LANE:(n_half + ct + 1) * LANE])
                part = term if part is None else part + term
            pbuf[pl.ds(row0, SUB), :] = part

        act = _gelu_exact(jnp.sum(pbuf[...], axis=1, keepdims=True))
        g_row = jnp.broadcast_to(g_ref[pl.ds(t, 1), :], (N_SEL, LANE))
        g_col = jnp.sum(jnp.where(eye, g_row, 0.0), axis=1, keepdims=True)
        cbuf[...] = jnp.broadcast_to(g_col * act, (N_SEL, LANE))

        acc = [jnp.zeros((SUB, LANE), jnp.float32)] * (2 * n_half)
        for j in range(N_CHUNK):
            start_rows(nxt, nslot, N_CHUNK + j)
            row0 = j * SUB
            cc = cbuf[pl.ds(row0, SUB), :]
            new = list(acc)
            for ct in range(n_half):
                lo, hi = _unpack(buf[slot, n_half + ct, pl.ds(row0, SUB), :])
                new[ct] = acc[ct] + lo * cc
                new[n_half + ct] = acc[n_half + ct] + hi * cc
            acc = new
        o_ref[pl.ds(t, 1), :] = res_ref[pl.ds(t, 1), :] + jnp.sum(jnp.concatenate(acc, axis=1), axis=0, keepdims=True)
        return c
    lax.fori_loop(0, tb, token, 0)

    wait_rows(tb % 2)


def peer_gather(res, xn, eid, gate, tab, tb):
    t, d = xn.shape
    assert t % tb == 0 and d % (2 * LANE) == 0
    kern = functools.partial(_peer_gather_kernel, tb=tb, d=d)
    n_tiles = d // LANE
    return pl.pallas_call(
        kern,
        grid=(t // tb,),
        in_specs=[pl.BlockSpec((tb, N_SEL), lambda i: (i, 0), memory_space=pltpu.SMEM),
                  pl.BlockSpec((tb, d), lambda i: (i, 0)),
                  pl.BlockSpec((tb, d), lambda i: (i, 0)),
                  pl.BlockSpec((tb, N_SEL), lambda i: (i, 0)),
                  pl.BlockSpec(memory_space=pl.ANY)],
        out_specs=pl.BlockSpec((tb, d), lambda i: (i, 0)),
        out_shape=jax.ShapeDtypeStruct((t, d), jnp.float32),
        scratch_shapes=[pltpu.VMEM((2, n_tiles, N_SEL, LANE), jnp.uint32),
                        pltpu.VMEM((N_SEL, n_tiles, LANE), jnp.uint32),
                        pltpu.VMEM((N_SEL, LANE), jnp.float32),
                        pltpu.VMEM((N_SEL, LANE), jnp.float32),
                        pltpu.SemaphoreType.DMA((2,))],
        compiler_params=pltpu.CompilerParams(
            dimension_semantics=("arbitrary",),
            vmem_limit_bytes=V7X_VMEM_LIMIT_BYTES),
        name="peer_gather",
    )(eid, res, xn, gate, tab)


def pack_tables(u_tab, v_tab):
    e, d = u_tab.shape

    def pk(t):
        bits = lax.bitcast_convert_type(t.astype(jnp.bfloat16), jnp.uint16).astype(jnp.uint32)
        return (bits[:, d // 2:] << 16) | bits[:, :d // 2]
    return jnp.concatenate([pk(u_tab), pk(v_tab)], axis=1).reshape(e, d // LANE, LANE)


def peer_route(q, sub_keys):
    t = q.shape[0]
    q = q.reshape(t, PEER_HEADS, 2, PEER_QDIM // 2)
    s = jnp.einsum('thpd,pkd->thpk', q, sub_keys, precision=lax.Precision.HIGHEST)
    s1, i1 = lax.top_k(s[:, :, 0], PEER_TOPK)
    s2, i2 = lax.top_k(s[:, :, 1], PEER_TOPK)
    cand = (s1[..., :, None] + s2[..., None, :]).reshape(t, PEER_HEADS, PEER_TOPK * PEER_TOPK)
    cid = (i1[..., :, None] * PEER_NKEYS + i2[..., None, :]).reshape(t, PEER_HEADS, PEER_TOPK * PEER_TOPK)
    sc, j = lax.top_k(cand, PEER_TOPK)
    eid = jnp.take_along_axis(cid, j, axis=-1)
    gate = jax.nn.softmax(sc, axis=-1)
    return eid.reshape(t, N_SEL).astype(jnp.int32), gate.reshape(t, N_SEL)


def peer_layer(y, g, w_q, sub_keys, tab):
    bsz, L, d = y.shape
    yf = y.reshape(bsz * L, d)
    xn = _rmsnorm(yf, g, jnp.float32)
    q = _matmul(xn.astype(jnp.bfloat16), w_q)
    eid, gate = peer_route(q, sub_keys)
    out = peer_gather(yf, xn, eid, gate, tab, min(PEER_TOK_BLOCK, bsz * L))
    return out.reshape(bsz, L, d)


def _norm3(x, g):
    bsz, L, d = x.shape
    return _rmsnorm(x.reshape(bsz * L, d), g, jnp.bfloat16).reshape(bsz, L, d)


def kernel(x_prompt, x_sample, cache_k, cache_v, state_delta, state_conv, page_table,
           norm_mix, norm_ffn, ab_w_in, ab_w_out, a_conv_w, a_log, a_dt_bias, a_out_norm,
           b_q_norm, b_k_norm, c_w_in, c_b_in, c_ln_g, c_ln_b, c_w_s, c_b_s, c_w_out,
           peer_w_q, peer_sub_keys, peer_u, peer_v):
    bp = x_prompt.shape[0]
    bs = x_sample.shape[0]
    past_len = page_table.shape[1] * cache_k.shape[2]
    bf16 = jnp.bfloat16
    yp, ys = x_prompt, x_sample
    kp_l, vp_l, dp_l, cp_l = [], [], [], []
    ks_l, vs_l, ds_l, cs_l = [], [], [], []
    sv_l = []
    for layer in range(DEPTH):
        hp = _norm3(yp, norm_mix[layer])
        hs = _norm3(ys, norm_mix[layer])
        if layer % 2 == 0:
            i = layer // 2
            w_in = ab_w_in[i]
            split = A_CONV_DIM + A_V_W
            w_small = jnp.pad(w_in[:, split:P_A], ((0, 0), (0, 128 - 2 * A_HEADS))).astype(bf16)
            w_in_parts = (w_in[:, :split].astype(bf16), w_small, w_in[:, P_A:].astype(bf16))
            w = (w_in_parts, ab_w_out[i].astype(bf16), a_conv_w[i], a_log[i], a_dt_bias[i], a_out_norm[i],
                 b_q_norm[i], b_k_norm[i])
            k_past = cache_k[i, page_table].reshape(bs, past_len, B_HEADS, B_DH)
            v_past = cache_v[i, page_table].reshape(bs, past_len, B_HEADS, B_DH)
            conv0 = jnp.zeros((bp, A_CONV - 1, A_CONV_DIM), x_prompt.dtype)
            s0 = jnp.zeros((bp, A_HEADS, A_DK, A_DV), jnp.float32)
            mp, kp, vp, cp, dp = ab_mixer(hp, 0, conv0, s0, [], [], *w)
            ms, kn, vn, cn, dn = ab_mixer(hs, past_len, state_conv[i], state_delta[i], [k_past], [v_past], *w)
            kp_l.append(kp); vp_l.append(vp); dp_l.append(dp); cp_l.append(cp)
            ks_l.append(kn); vs_l.append(vn); ds_l.append(dn); cs_l.append(cn)
        else:
            j = layer // 2
            w = (c_w_in[j].astype(bf16), c_b_in[j], c_ln_g[j], c_ln_b[j], c_w_s[j], c_b_s[j], c_w_out[j].astype(bf16))
            mp, _ = c_mixer(hp, *w)
            ms, sv = c_mixer(hs, *w)
            sv_l.append(sv)
        yp = yp + mp
        ys = ys + ms
        pw = (peer_w_q[layer].astype(bf16), peer_sub_keys[layer], pack_tables(peer_u[layer], peer_v[layer]))
        yp = peer_layer(yp, norm_ffn[layer], *pw)
        ys = peer_layer(ys, norm_ffn[layer], *pw)
    return (yp, ys, jnp.stack(kp_l), jnp.stack(vp_l), jnp.stack(dp_l), jnp.stack(cp_l),
            jnp.stack(ks_l), jnp.stack(vs_l), jnp.stack(ds_l), jnp.stack(cs_l), jnp.stack(sv_l))
```

```python
import functools
import math

import jax
import jax.numpy as jnp
from jax import lax
from jax.experimental import pallas as pl
from jax.experimental.pallas import tpu as pltpu

D_MODEL = 4096
DEPTH = 2

A_HEADS = 16
A_DK = 128
A_DV = 128
A_CONV = 4
A_CHUNK = 64
A_QK_W = A_HEADS * A_DK
A_V_W = A_HEADS * A_DV
A_CONV_DIM = 2 * A_QK_W + A_V_W
P_A = A_CONV_DIM + A_V_W + 2 * A_HEADS

B_HEADS = 16
B_DH = 128
B_W = B_HEADS * B_DH
MOBA_BLOCK = 256
MOBA_TOPK = 3
ROPE_THETA = 500000.0
ROPE_DIM = B_DH // 4
Q_BLOCK = 128

C_WIDTH = D_MODEL
C_GROUPS = 16
C_GROUP_W = C_WIDTH // C_GROUPS
C_CHUNK = 128

PEER_HEADS = 8
PEER_NKEYS = 128
PEER_QDIM = 256
PEER_TOPK = 16
PEER_TOK_BLOCK = 128

NORM_EPS = 1e-6
LN_EPS = 1e-5

V7X_VMEM_LIMIT_BYTES = 48 * 1024 * 1024


INV_SQRT2 = 0.7071067811865476


def _gelu_exact(x):
    return 0.5 * x * (1.0 + lax.erf(x * INV_SQRT2))


def _mm_kernel(a_ref, b_ref, o_ref):
    o_ref[...] = jnp.dot(a_ref[...], b_ref[...], preferred_element_type=jnp.float32)


def _mm_res_kernel(a_ref, b_ref, r_ref, o_ref):
    o_ref[...] = r_ref[...] + jnp.dot(a_ref[...], b_ref[...], preferred_element_type=jnp.float32)


def _mm_bias_gelu_kernel(a_ref, b_ref, bias_ref, o_ref):
    o_ref[...] = _gelu_exact(jnp.dot(a_ref[...], b_ref[...], preferred_element_type=jnp.float32) + bias_ref[...])


def _matmul(a, b, residual=None, gelu_bias=None):
    m, k = a.shape
    n = b.shape[1]
    tm = min(m, 512)
    tn = min(n, 1024)
    assert m % tm == 0 and n % tn == 0, (m, n)
    assert residual is None or gelu_bias is None
    grid = (n // tn, m // tm)
    in_specs = [pl.BlockSpec((tm, k), lambda j, i: (i, 0)),
                pl.BlockSpec((k, tn), lambda j, i: (0, j))]
    args = [a, b]
    body = _mm_kernel
    if residual is not None:
        in_specs.append(pl.BlockSpec((tm, tn), lambda j, i: (i, j)))
        args.append(residual)
        body = _mm_res_kernel
    if gelu_bias is not None:
        in_specs.append(pl.BlockSpec((1, tn), lambda j, i: (0, j)))
        args.append(gelu_bias.reshape(1, n))
        body = _mm_bias_gelu_kernel
    return pl.pallas_call(
        body,
        grid=grid,
        in_specs=in_specs,
        out_specs=pl.BlockSpec((tm, tn), lambda j, i: (i, j)),
        out_shape=jax.ShapeDtypeStruct((m, n), jnp.float32),
        compiler_params=pltpu.CompilerParams(
            dimension_semantics=("arbitrary", "arbitrary"),
            vmem_limit_bytes=V7X_VMEM_LIMIT_BYTES),
        name="matmul",
    )(*args)


def _rms_kernel(x_ref, g_ref, o_ref):
    x = x_ref[...]
    y = x * lax.rsqrt(jnp.mean(x * x, axis=-1, keepdims=True) + NORM_EPS)
    o_ref[...] = (y * g_ref[...]).astype(o_ref.dtype)


def _rmsnorm(x, g, out_dtype):
    t, d = x.shape
    tr = min(t, 256)
    assert t % tr == 0
    return pl.pallas_call(
        _rms_kernel,
        grid=(t // tr,),
        in_specs=[pl.BlockSpec((tr, d), lambda i: (i, 0)),
                  pl.BlockSpec((1, d), lambda i: (0, 0))],
        out_specs=pl.BlockSpec((tr, d), lambda i: (i, 0)),
        out_shape=jax.ShapeDtypeStruct((t, d), out_dtype),
        compiler_params=pltpu.CompilerParams(dimension_semantics=("arbitrary",)),
        name="rmsnorm",
    )(x, g.reshape(1, d))


def rms_norm(x, g):
    xf = x.astype(jnp.float32)
    y = xf * lax.rsqrt(jnp.mean(xf * xf, axis=-1, keepdims=True) + NORM_EPS)
    return (y * g.astype(jnp.float32)).astype(x.dtype)


def layer_norm(x, g, b):
    xf = x.astype(jnp.float32)
    mu = jnp.mean(xf, axis=-1, keepdims=True)
    xc = xf - mu
    y = xc * lax.rsqrt(jnp.mean(xc * xc, axis=-1, keepdims=True) + LN_EPS)
    return (y * g.astype(jnp.float32) + b.astype(jnp.float32)).astype(x.dtype)


def l2_norm(x):
    xf = x.astype(jnp.float32)
    return xf * lax.rsqrt(jnp.sum(xf * xf, axis=-1, keepdims=True) + NORM_EPS)


def partial_rope(x, pos):
    half = ROPE_DIM // 2
    inv_freq = ROPE_THETA ** (-jnp.arange(half, dtype=jnp.float32) / half)
    ang = pos.astype(jnp.float32)[:, None] * inv_freq[None, :]
    cos = jnp.cos(ang)[:, None, :]
    sin = jnp.sin(ang)[:, None, :]
    xr = x[..., :ROPE_DIM].astype(jnp.float32)
    x1, x2 = xr[..., :half], xr[..., half:]
    rot = jnp.concatenate([x1 * cos - x2 * sin, x2 * cos + x1 * sin], axis=-1)
    return jnp.concatenate([rot.astype(x.dtype), x[..., ROPE_DIM:]], axis=-1)


def gated_delta_chunked(q, k, v, g, beta, s0):
    bsz, L, H, dk = q.shape
    dv = v.shape[-1]
    c = min(A_CHUNK, L)
    n = -(-L // c)
    pad = n * c - L

    def prep(t):
        t = t.astype(jnp.float32)
        t = jnp.pad(t, [(0, 0), (0, pad)] + [(0, 0)] * (t.ndim - 2))
        t = t.reshape((bsz, n, c) + t.shape[2:])
        return jnp.moveaxis(t, (1, 2), (0, 3))

    qc = prep(q) * dk ** -0.5
    kc, vc, gc, bc = prep(k), prep(v), prep(g), prep(beta)
    G = jnp.cumsum(gc, axis=-1)
    tri = jnp.tril(jnp.ones((c, c), bool))
    strict_tri = jnp.tril(jnp.ones((c, c), bool), -1)
    diff = G[..., :, None] - G[..., None, :]
    decay = jnp.where(tri, jnp.exp(jnp.where(tri, diff, 0.0)), 0.0)
    kbeta = kc * bc[..., None]
    lower = jnp.where(strict_tri, jnp.einsum('nbhid,nbhjd->nbhij', kbeta, kc) * decay, 0.0)
    hi = lax.Precision.HIGHEST
    t_inv = jnp.eye(c, dtype=jnp.float32) - lower
    pw2 = lower
    for _ in range(max(c - 1, 0).bit_length() - 1):
        pw2 = jnp.matmul(pw2, pw2, precision=hi)
        t_inv = t_inv + jnp.matmul(t_inv, pw2, precision=hi)
    u = t_inv @ (vc * bc[..., None])
    w = t_inv @ (kbeta * jnp.exp(G)[..., None])
    qk = jnp.where(tri, jnp.einsum('nbhid,nbhjd->nbhij', qc, kc) * decay, 0.0)

    def step(S, xs):
        qi, ki, ui, wi, Gi, qki = xs
        v_new = ui - wi @ S
        o = (qi * jnp.exp(Gi)[..., None]) @ S + qki @ v_new
        g_last = Gi[..., -1]
        S = S * jnp.exp(g_last)[..., None, None] + jnp.einsum(
            'bhcd,bhce->bhde', ki * jnp.exp(g_last[..., None] - Gi)[..., None], v_new)
        return S, o

    S, o = lax.scan(step, s0.astype(jnp.float32), (qc, kc, u, w, G, qk))
    o = jnp.moveaxis(o, (0, 3), (1, 2)).reshape(bsz, n * c, H, dv)[:, :L]
    return o, S.astype(s0.dtype)


DELTA_HEADS_PER_STEP = 4


def _dot(a, b):
    return jnp.dot(a, b, precision=lax.Precision.HIGHEST, preferred_element_type=jnp.float32)


def _dot_nt(a, b):
    return lax.dot_general(a, b, (((1,), (1,)), ((), ())), precision=lax.Precision.HIGHEST,
                           preferred_element_type=jnp.float32)


def _dot_tn(a, b):
    return lax.dot_general(a, b, (((0,), (0,)), ((), ())), precision=lax.Precision.HIGHEST,
                           preferred_element_type=jnp.float32)


def _delta_kernel(q_ref, k_ref, v_ref, g_ref, b_ref, s0_ref, o_ref, s_ref, *, n_chunks, heads):
    c = A_CHUNK
    ri = lax.broadcasted_iota(jnp.int32, (c, c), 0)
    ci = lax.broadcasted_iota(jnp.int32, (c, c), 1)
    tril = ri >= ci
    strict = ri > ci
    eye = ri == ci
    eye_f = eye.astype(jnp.float32)

    def to_col(row):
        return jnp.sum(jnp.where(eye, jnp.broadcast_to(row, (c, c)), 0.0), axis=1, keepdims=True)

    def chunk(n, states):
        r0 = pl.multiple_of(n * c, c)
        hs = range(heads)
        cols = [slice(h * A_DK, (h + 1) * A_DK) for h in hs]
        qc = [q_ref[0, pl.ds(r0, c), cols[h]] * (A_DK ** -0.5) for h in hs]
        kc = [k_ref[0, pl.ds(r0, c), cols[h]] for h in hs]
        vc = [v_ref[0, pl.ds(r0, c), cols[h]] for h in hs]
        g_row = [g_ref[0, h, pl.ds(n, 1), :] for h in hs]
        beta = [to_col(b_ref[0, h, pl.ds(n, 1), :]) for h in hs]
        G_col = [jnp.sum(jnp.where(tril, jnp.broadcast_to(g_row[h], (c, c)), 0.0), axis=1, keepdims=True) for h in hs]
        G_row = [jnp.sum(jnp.where(eye, jnp.broadcast_to(G_col[h], (c, c)), 0.0), axis=0, keepdims=True) for h in hs]
        decay = [jnp.where(tril, jnp.exp(jnp.where(tril, G_col[h] - G_row[h], 0.0)), 0.0) for h in hs]
        kbeta = [kc[h] * beta[h] for h in hs]
        low = [jnp.where(strict, _dot_nt(kbeta[h], kc[h]) * decay[h], 0.0) for h in hs]
        qk = [_dot_nt(qc[h], kc[h]) * decay[h] for h in hs]
        t_inv = [eye_f - low[h] for h in hs]
        pw = low
        for _ in range((c - 1).bit_length() - 1):
            pw = [_dot(pw[h], pw[h]) for h in hs]
            t_inv = [t_inv[h] + _dot(t_inv[h], pw[h]) for h in hs]
        eG = [jnp.exp(G_col[h]) for h in hs]
        u = [_dot(t_inv[h], vc[h] * beta[h]) for h in hs]
        w = [_dot(t_inv[h], kbeta[h] * eG[h]) for h in hs]
        v_new = [u[h] - _dot(w[h], states[h]) for h in hs]
        qs = [_dot(qc[h] * eG[h], states[h]) for h in hs]
        out = [qs[h] + _dot(qk[h], v_new[h]) for h in hs]
        g_last = [jnp.sum(g_row[h], axis=1, keepdims=True) for h in hs]
        new_states = [states[h] * jnp.exp(g_last[h]) + _dot_tn(kc[h] * jnp.exp(g_last[h] - G_col[h]), v_new[h])
                      for h in hs]
        for h in hs:
            o_ref[0, pl.ds(r0, c), cols[h]] = out[h]
        return tuple(new_states)

    init = tuple(s0_ref[0, h] for h in range(heads))
    final = lax.fori_loop(0, n_chunks, chunk, init)
    for h in range(heads):
        s_ref[0, h] = final[h]


def gated_delta(q, k, v, g, beta, s0, heads_per_step):
    bsz, L, w = q.shape
    H = w // A_DK
    assert L % A_CHUNK == 0 and H % heads_per_step == 0
    hb = heads_per_step
    kern = functools.partial(_delta_kernel, n_chunks=L // A_CHUNK, heads=hb)
    seq = pl.BlockSpec((1, L, hb * A_DK), lambda b, h: (b, 0, h))
    gb = pl.BlockSpec((1, hb, L // A_CHUNK, A_CHUNK), lambda b, h: (b * (H // hb) + h, 0, 0, 0))
    st = pl.BlockSpec((1, hb, A_DK, A_DV), lambda b, h: (b, h, 0, 0))
    grp = lambda a: a.reshape(bsz * (H // hb), hb, L // A_CHUNK, A_CHUNK)
    return pl.pallas_call(
        kern,
        grid=(bsz, H // hb),
        in_specs=[seq, seq, seq, gb, gb, st],
        out_specs=[seq, st],
        out_shape=[jax.ShapeDtypeStruct((bsz, L, H * A_DV), jnp.float32),
                   jax.ShapeDtypeStruct((bsz, H, A_DK, A_DV), jnp.float32)],
        compiler_params=pltpu.CompilerParams(
            dimension_semantics=("arbitrary", "arbitrary"),
            vmem_limit_bytes=V7X_VMEM_LIMIT_BYTES),
        name="gated_delta",
    )(q, k, v, grp(g), grp(beta), s0)


def mixer_a(pa, conv_buf, s0, conv_w, a_log, dt_bias, out_norm):
    bsz, L, _ = pa.shape
    qkv = pa[..., :A_CONV_DIM]
    z = pa[..., A_CONV_DIM:A_CONV_DIM + A_V_W]
    b_logit = pa[..., A_CONV_DIM + A_V_W:A_CONV_DIM + A_V_W + A_HEADS]
    a_in = pa[..., A_CONV_DIM + A_V_W + A_HEADS:]
    xp = jnp.concatenate([conv_buf.astype(qkv.dtype), qkv], axis=1)
    conv = jax.nn.silu(sum(xp[:, j:j + L] * conv_w[:, j] for j in range(A_CONV)))
    new_buf = xp[:, L:]
    q = l2_norm(conv[..., :A_QK_W].reshape(bsz, L, A_HEADS, A_DK))
    k = l2_norm(conv[..., A_QK_W:2 * A_QK_W].reshape(bsz, L, A_HEADS, A_DK))
    v = conv[..., 2 * A_QK_W:].reshape(bsz, L, A_HEADS, A_DV)
    beta = jax.nn.sigmoid(b_logit.astype(jnp.float32))
    g = -jnp.exp(a_log.astype(jnp.float32)) * jax.nn.softplus(a_in.astype(jnp.float32) + dt_bias.astype(jnp.float32))
    if L % A_CHUNK == 0:
        flat = lambda a: a.reshape(bsz, L, -1)
        o, s_new = gated_delta(flat(q), flat(k), flat(v), jnp.moveaxis(g, 1, 2), jnp.moveaxis(beta, 1, 2),
                               s0, DELTA_HEADS_PER_STEP)
        o = o.reshape(bsz, L, A_HEADS, A_DV)
    else:
        o, s_new = gated_delta_chunked(q, k, v, g, beta, s0)
    o = rms_norm(o, out_norm) * jax.nn.silu(z.reshape(bsz, L, A_HEADS, A_DV).astype(jnp.float32))
    return o.reshape(bsz, L, A_V_W).astype(pa.dtype), new_buf, s_new


def _kmean_kernel(pt_ref, k_ref, o_ref, *, pages_per_block, page_size):
    p = pl.program_id(1)
    part = jnp.sum(k_ref[0, 0], axis=0)

    @pl.when(p % pages_per_block == 0)
    def _():
        o_ref[0, 0] = part

    @pl.when(p % pages_per_block != 0)
    def _():
        o_ref[0, 0] += part

    @pl.when(p % pages_per_block == pages_per_block - 1)
    def _():
        o_ref[0, 0] = o_ref[0, 0] * (1.0 / (pages_per_block * page_size))


def moba_past_kmean(cache_k, layer, page_table):
    _, _, page_size, n_heads, dh = cache_k.shape
    bsz, n_pages = page_table.shape
    ppb = MOBA_BLOCK // page_size
    assert MOBA_BLOCK % page_size == 0 and n_pages % ppb == 0
    kern = functools.partial(_kmean_kernel, pages_per_block=ppb, page_size=page_size)
    return pl.pallas_call(
        kern,
        grid_spec=pltpu.PrefetchScalarGridSpec(
            num_scalar_prefetch=1,
            grid=(bsz, n_pages),
            in_specs=[pl.BlockSpec((1, 1, page_size, n_heads, dh), lambda b, p, pt: (layer, pt[b, p], 0, 0, 0))],
            out_specs=pl.BlockSpec((1, 1, n_heads, dh), lambda b, p, pt: (b, p // ppb, 0, 0))),
        out_shape=jax.ShapeDtypeStruct((bsz, n_pages // ppb, n_heads, dh), jnp.float32),
        compiler_params=pltpu.CompilerParams(dimension_semantics=("arbitrary", "arbitrary")),
        name="moba_past_kmean",
    )(page_table, cache_k)


def _moba_decode_kernel(pg_ref, q_ref, kn_ref, vn_ref, ck_hbm, cv_hbm, o_ref, kbuf, vbuf, sem, *,
                        layer, n_sel_pages, page_size):
    b = pl.program_id(0)
    h = pl.program_id(1)

    def page_copy(src, dst, s, which):
        return pltpu.make_async_copy(src.at[layer, pg_ref[b, h, s], :, h, :],
                                     dst.at[pl.ds(s * page_size, page_size), :], sem.at[which, s])

    for s in range(n_sel_pages):
        page_copy(ck_hbm, kbuf, s, 0).start()
        page_copy(cv_hbm, vbuf, s, 1).start()

    scale = B_DH ** -0.5
    q = q_ref[0, pl.ds(h, 1), :] * scale
    kn = kn_ref[0, pl.ds(h, 1), :]
    vn = vn_ref[0, pl.ds(h, 1), :]
    s_own = jnp.sum(q * kn, axis=1, keepdims=True)

    for s in range(n_sel_pages):
        page_copy(ck_hbm, kbuf, s, 0).wait()
        page_copy(cv_hbm, vbuf, s, 1).wait()

    sc = lax.dot_general(q, kbuf[...], (((1,), (1,)), ((), ())), precision=lax.Precision.HIGHEST,
                         preferred_element_type=jnp.float32)
    m = jnp.maximum(jnp.max(sc, axis=1, keepdims=True), s_own)
    p = jnp.exp(sc - m)
    p_own = jnp.exp(s_own - m)
    l = jnp.sum(p, axis=1, keepdims=True) + p_own
    acc = jnp.dot(p, vbuf[...], precision=lax.Precision.HIGHEST, preferred_element_type=jnp.float32) + p_own * vn
    o_ref[0, pl.ds(h, 1), :] = acc / l


def moba_decode(q, k_new, v_new, sel_pages, cache_k, cache_v, layer):
    bsz, n_heads, dh = q.shape
    page_size = cache_k.shape[2]
    n_sel_pages = sel_pages.shape[-1]
    kern = functools.partial(_moba_decode_kernel, layer=layer, n_sel_pages=n_sel_pages, page_size=page_size)
    row = pl.BlockSpec((1, n_heads, dh), lambda b, h, pg: (b, 0, 0))
    return pl.pallas_call(
        kern,
        grid_spec=pltpu.PrefetchScalarGridSpec(
            num_scalar_prefetch=1,
            grid=(bsz, n_heads),
            in_specs=[row, row, row, pl.BlockSpec(memory_space=pl.ANY), pl.BlockSpec(memory_space=pl.ANY)],
            out_specs=row,
            scratch_shapes=[pltpu.VMEM((n_sel_pages * page_size, dh), jnp.float32),
                            pltpu.VMEM((n_sel_pages * page_size, dh), jnp.float32),
                            pltpu.SemaphoreType.DMA((2, n_sel_pages))]),
        out_shape=jax.ShapeDtypeStruct((bsz, n_heads, dh), jnp.float32),
        compiler_params=pltpu.CompilerParams(dimension_semantics=("arbitrary", "arbitrary")),
        name="moba_decode",
    )(sel_pages, q, k_new, v_new, cache_k, cache_v)


def moba_sample(q, k_new, v_new, cache_k, cache_v, layer, page_table):
    page_size = cache_k.shape[2]
    ppb = MOBA_BLOCK // page_size
    n_blocks = page_table.shape[1] // ppb
    assert (page_table.shape[1] * page_size) % MOBA_BLOCK == 0 and n_blocks >= MOBA_TOPK
    kmean = moba_past_kmean(cache_k, layer, page_table)
    gate = jnp.einsum('bhd,bnhd->bhn', q, kmean, precision=lax.Precision.HIGHEST)
    _, sel = lax.top_k(gate, MOBA_TOPK)
    blk_pages = page_table.reshape(page_table.shape[0], n_blocks, ppb)
    sel_pages = jnp.take_along_axis(blk_pages[:, None], sel[..., None], axis=2)
    sel_pages = sel_pages.reshape(sel.shape[0], sel.shape[1], MOBA_TOPK * ppb).astype(jnp.int32)
    return moba_decode(q, k_new, v_new, sel_pages, cache_k, cache_v, layer)


def _project(h, w_bf16):
    bsz, L, d = h.shape
    return _matmul(h.reshape(bsz * L, d).astype(jnp.bfloat16), w_bf16).reshape(bsz, L, -1)


def ab_mixer(y, h, pos0, conv_buf, s0, past, w_in, w_out, conv_w, a_log, dt_bias, out_norm, q_norm, k_norm):
    bsz, L, _ = h.shape
    w_main, w_small, w_b = w_in
    p_main = _project(h, w_main)
    p_small = _project(h, w_small)[..., :2 * A_HEADS]
    pb = _project(h, w_b)
    pa = jnp.concatenate([p_main, p_small], axis=-1)
    oa, conv_new, s_new = mixer_a(pa, conv_buf, s0, conv_w, a_log, dt_bias, out_norm)
    pos = pos0 + jnp.arange(L, dtype=jnp.int32)
    q = partial_rope(rms_norm(pb[..., :B_W].reshape(bsz, L, B_HEADS, B_DH), q_norm), pos)
    k = partial_rope(rms_norm(pb[..., B_W:2 * B_W].reshape(bsz, L, B_HEADS, B_DH), k_norm), pos)
    v = pb[..., 2 * B_W:].reshape(bsz, L, B_HEADS, B_DH)
    if past is not None:
        assert L == 1, "the cached path takes one new token per sequence"
        ob = moba_sample(q[:, 0], k[:, 0], v[:, 0], *past).reshape(bsz, L, B_W)
    else:
        assert pos0 == 0
        ob = moba_prompt(q.reshape(bsz, L, B_W), k.reshape(bsz, L, B_W), v.reshape(bsz, L, B_W), B_HEADS)
    cat = jnp.concatenate([oa.astype(jnp.bfloat16), ob.astype(jnp.bfloat16)], axis=-1)
    d = y.shape[-1]
    y = _matmul(cat.reshape(bsz * L, -1), w_out, residual=y.reshape(bsz * L, d)).reshape(bsz, L, d)
    return y, k, v, conv_new, s_new


def _sgu_kernel(u_ref, v_ref, lng_ref, lnb_ref, ws_ref, bst_ref, o_ref):
    v = v_ref[...]
    mu = jnp.mean(v, axis=-1, keepdims=True)
    vc = v - mu
    vn = vc * lax.rsqrt(jnp.mean(vc * vc, axis=-1, keepdims=True) + LN_EPS) * lng_ref[...] + lnb_ref[...]
    c = v.shape[0]
    causal = lax.broadcasted_iota(jnp.int32, (c, c), 0) >= lax.broadcasted_iota(jnp.int32, (c, c), 1)
    for g in range(C_GROUPS):
        cols = slice(g * C_GROUP_W, (g + 1) * C_GROUP_W)
        ws = jnp.where(causal, ws_ref[g], 0.0).astype(jnp.bfloat16)
        mix = jnp.dot(ws, vn[:, cols].astype(jnp.bfloat16), preferred_element_type=jnp.float32)
        o_ref[:, cols] = (u_ref[:, cols] * (mix + bst_ref[:, g:g + 1])).astype(o_ref.dtype)


def _sgu_chunks(z, ln_g, ln_b, w_s, b_s):
    t = z.shape[0]
    assert t % C_CHUNK == 0
    row = lambda a: a.reshape(1, C_WIDTH)
    return pl.pallas_call(
        _sgu_kernel,
        grid=(t // C_CHUNK,),
        in_specs=[pl.BlockSpec((C_CHUNK, C_WIDTH), lambda i: (i, 0)),
                  pl.BlockSpec((C_CHUNK, C_WIDTH), lambda i: (i, 1)),
                  pl.BlockSpec((1, C_WIDTH), lambda i: (0, 0)),
                  pl.BlockSpec((1, C_WIDTH), lambda i: (0, 0)),
                  pl.BlockSpec((C_GROUPS, C_CHUNK, C_CHUNK), lambda i: (0, 0, 0)),
                  pl.BlockSpec((C_CHUNK, C_GROUPS), lambda i: (0, 0))],
        out_specs=pl.BlockSpec((C_CHUNK, C_WIDTH), lambda i: (i, 0)),
        out_shape=jax.ShapeDtypeStruct((t, C_WIDTH), jnp.bfloat16),
        compiler_params=pltpu.CompilerParams(dimension_semantics=("arbitrary",)),
        name="sgu",
    )(z, z, row(ln_g), row(ln_b), w_s, b_s.T)


def c_mixer(y, h, w_in, b_in, ln_g, ln_b, w_s, b_s, w_out):
    bsz, L, d = h.shape
    z = _matmul(h.reshape(bsz * L, d), w_in, gelu_bias=b_in)
    if L % C_CHUNK == 0:
        gated = _sgu_chunks(z, ln_g, ln_b, w_s, b_s)
        v = None
    else:
        z = z.reshape(bsz, L, -1)
        u, v = z[..., :C_WIDTH], z[..., C_WIDTH:]
        v = layer_norm(v, ln_g, ln_b)
        c = min(C_CHUNK, L)
        n = -(-L // c)
        lp = n * c
        vp = jnp.pad(v, ((0, 0), (0, lp - L), (0, 0))).reshape(bsz, n, c, C_GROUPS, C_GROUP_W)
        ws = jnp.tril(w_s[:, :c, :c])
        mix = jnp.einsum('gij,bnjgc->bnigc', ws, vp) + b_s[:, :c].T[None, None, :, :, None]
        mix = mix.reshape(bsz, lp, C_WIDTH)[:, :L]
        gated = (u * mix).reshape(bsz * L, C_WIDTH).astype(jnp.bfloat16)
    y = _matmul(gated, w_out, residual=y.reshape(bsz * L, d)).reshape(bsz, L, d)
    return y, v


NEG_INF = float("-inf")


def _moba_prompt_kernel(q_ref, k_ref, v_ref, o_ref, kbf, vbf, kmean, *, n_blocks):
    i = pl.program_id(2)
    blk = MOBA_BLOCK

    @pl.when(i == 0)
    def _():
        k = k_ref[0]
        kbf[...] = k.astype(jnp.bfloat16)
        vbf[...] = v_ref[0].astype(jnp.bfloat16)
        kmean[...] = jnp.mean(k.reshape(n_blocks, blk, B_DH), axis=1)

    own = i
    q = q_ref[0]
    gate = lax.dot_general(q, kmean[...], (((1,), (1,)), ((), ())),
                           precision=lax.Precision.HIGHEST,
                           preferred_element_type=jnp.float32)
    lane = lax.broadcasted_iota(jnp.int32, gate.shape, 1)
    avail = lane < own
    sel = jnp.zeros(gate.shape, jnp.bool_)
    for _ in range(MOBA_TOPK):
        g = jnp.where(avail, gate, NEG_INF)
        mx = jnp.max(g, axis=1, keepdims=True)
        first = jnp.min(jnp.where(avail & (g == mx), lane, n_blocks), axis=1, keepdims=True)
        pick = lane == first
        sel = sel | pick
        avail = avail & jnp.logical_not(pick)
    sel_f = sel.astype(jnp.float32)

    qs = (q * (B_DH ** -0.5)).astype(jnp.bfloat16)
    nt = (((1,), (1,)), ((), ()))

    start = pl.multiple_of(own * blk, blk)
    s = lax.dot_general(qs, kbf[pl.ds(start, blk), :], nt, preferred_element_type=jnp.float32)
    row = lax.broadcasted_iota(jnp.int32, s.shape, 0)
    col = lax.broadcasted_iota(jnp.int32, s.shape, 1)
    s = jnp.where(col <= row, s, NEG_INF)
    m0 = jnp.max(s, axis=1, keepdims=True)
    p = jnp.exp(s - m0)
    l0 = jnp.sum(p, axis=1, keepdims=True)
    acc0 = jnp.dot(p.astype(jnp.bfloat16), vbf[pl.ds(start, blk), :], preferred_element_type=jnp.float32)

    def body(n, carry):
        m, l, acc = carry
        st = pl.multiple_of(n * blk, blk)
        s = lax.dot_general(qs, kbf[pl.ds(st, blk), :], nt, preferred_element_type=jnp.float32)
        sel_n = jnp.sum(jnp.where(lane == n, sel_f, 0.0), axis=1, keepdims=True) > 0.0
        s = jnp.where(sel_n, s, NEG_INF)
        m_new = jnp.maximum(m, jnp.max(s, axis=1, keepdims=True))
        corr = jnp.exp(m - m_new)
        p = jnp.exp(s - m_new)
        l = l * corr + jnp.sum(p, axis=1, keepdims=True)
        acc = acc * corr + jnp.dot(p.astype(jnp.bfloat16), vbf[pl.ds(st, blk), :],
                                   preferred_element_type=jnp.float32)
        return m_new, l, acc

    m, l, acc = lax.fori_loop(0, own, body, (m0, l0, acc0))
    o_ref[0] = (acc / l).astype(o_ref.dtype)


def moba_prompt(q, k, v, n_heads):
    bsz, L, _ = q.shape
    assert L % MOBA_BLOCK == 0
    n_blocks = L // MOBA_BLOCK
    kern = functools.partial(_moba_prompt_kernel, n_blocks=n_blocks)
    return pl.pallas_call(
        kern,
        grid=(bsz, n_heads, n_blocks),
        in_specs=[pl.BlockSpec((1, MOBA_BLOCK, B_DH), lambda b, h, i: (b, i, h)),
                  pl.BlockSpec((1, L, B_DH), lambda b, h, i: (b, 0, h)),
                  pl.BlockSpec((1, L, B_DH), lambda b, h, i: (b, 0, h))],
        out_specs=pl.BlockSpec((1, MOBA_BLOCK, B_DH), lambda b, h, i: (b, i, h)),
        out_shape=jax.ShapeDtypeStruct(q.shape, jnp.bfloat16),
        scratch_shapes=[pltpu.VMEM((L, B_DH), jnp.bfloat16),
                        pltpu.VMEM((L, B_DH), jnp.bfloat16),
                        pltpu.VMEM((n_blocks, B_DH), jnp.float32)],
        compiler_params=pltpu.CompilerParams(
            dimension_semantics=("arbitrary", "arbitrary", "arbitrary")),
        name="moba_prompt",
    )(q, k, v)


N_SEL = PEER_HEADS * PEER_TOPK
SUB = 8
LANE = 128
PEER_AHEAD = 2
PEER_SLOTS = PEER_AHEAD + 1


def _peer_gather_kernel(eid_ref, res_ref, x_ref, g_ref, tab_hbm, o_ref,
                        buf, wbuf, pbuf, cbuf, sem, *, tb, rows):
    n_vreg = rows // SUB
    half = N_SEL // 2

    def start_rows(tok, slot, k0, k1):
        for k in range(k0, k1):
            pltpu.make_async_copy(tab_hbm.at[eid_ref[tok, k]], buf.at[slot, k], sem.at[slot]).start()

    def wait_rows(slot):
        pltpu.make_async_copy(tab_hbm.at[pl.ds(0, N_SEL)], wbuf, sem.at[slot]).wait()

    for ahead in range(PEER_AHEAD):
        start_rows(min(ahead, tb - 1), ahead, 0, N_SEL)

    eye = (lax.broadcasted_iota(jnp.int32, (N_SEL, LANE), 0)
           == lax.broadcasted_iota(jnp.int32, (N_SEL, LANE), 1))

    def token(t, c):
        slot = t % PEER_SLOTS
        nslot = (t + PEER_AHEAD) % PEER_SLOTS
        nxt = jnp.minimum(t + PEER_AHEAD, tb - 1)
        x = x_ref[t]
        xs = [x[i * SUB:(i + 1) * SUB, :] for i in range(n_vreg)]

        wait_rows(slot)

        for k in range(N_SEL):
            if k % SUB == 0:
                start_rows(nxt, nslot, k // 2, k // 2 + SUB // 2)
            u = buf[slot, k, 0:rows, :].astype(jnp.float32)
            part = u[0:SUB, :] * xs[0]
            for i in range(1, n_vreg):
                part = part + u[i * SUB:(i + 1) * SUB, :] * xs[i]
            pbuf[pl.ds(k, 1), :] = jnp.sum(part, axis=0, keepdims=True)

        act = _gelu_exact(jnp.sum(pbuf[...], axis=1, keepdims=True))
        g_row = jnp.broadcast_to(g_ref[pl.ds(t, 1), :], (N_SEL, LANE))
        g_col = jnp.sum(jnp.where(eye, g_row, 0.0), axis=1, keepdims=True)
        cbuf[...] = jnp.broadcast_to(g_col * act, (N_SEL, LANE))

        acc = [jnp.zeros((SUB, LANE), jnp.float32)] * n_vreg
        for k in range(N_SEL):
            if k % SUB == 0:
                start_rows(nxt, nslot, half + k // 2, half + k // 2 + SUB // 2)
            v = buf[slot, k, rows:2 * rows, :].astype(jnp.float32)
            coef = jnp.broadcast_to(cbuf[pl.ds(k, 1), :], (SUB, LANE))
            acc = [acc[i] + v[i * SUB:(i + 1) * SUB, :] * coef for i in range(n_vreg)]
        o_ref[t] = res_ref[t] + jnp.concatenate(acc, axis=0)
        return c
    lax.fori_loop(0, tb, token, 0)

    for ahead in range(PEER_AHEAD):
        wait_rows((tb + ahead) % PEER_SLOTS)


def peer_gather(res, xn, eid, gate, tab, tb):
    t, d = xn.shape
    rows = d // LANE
    assert t % tb == 0 and d % (LANE * 16) == 0 and tab.shape[1:] == (2 * rows, LANE)
    kern = functools.partial(_peer_gather_kernel, tb=tb, rows=rows)
    as3 = lambda a: a.reshape(t, rows, LANE)
    blk3 = pl.BlockSpec((tb, rows, LANE), lambda i: (i, 0, 0))
    out = pl.pallas_call(
        kern,
        grid=(t // tb,),
        in_specs=[pl.BlockSpec((tb, N_SEL), lambda i: (i, 0), memory_space=pltpu.SMEM),
                  blk3, blk3,
                  pl.BlockSpec((tb, N_SEL), lambda i: (i, 0)),
                  pl.BlockSpec(memory_space=pl.ANY)],
        out_specs=blk3,
        out_shape=jax.ShapeDtypeStruct((t, rows, LANE), jnp.float32),
        scratch_shapes=[pltpu.VMEM((PEER_SLOTS, N_SEL, 2 * rows, LANE), tab.dtype),
                        pltpu.VMEM((N_SEL, 2 * rows, LANE), tab.dtype),
                        pltpu.VMEM((N_SEL, LANE), jnp.float32),
                        pltpu.VMEM((N_SEL, LANE), jnp.float32),
                        pltpu.SemaphoreType.DMA((PEER_SLOTS,))],
        compiler_params=pltpu.CompilerParams(
            dimension_semantics=("arbitrary",),
            vmem_limit_bytes=V7X_VMEM_LIMIT_BYTES),
        name="peer_gather",
    )(eid, as3(res), as3(xn), gate, tab)
    return out.reshape(t, d)


def pack_tables(u_tab, v_tab):
    e, d = u_tab.shape
    slab = lambda a: a.reshape(e, d // LANE, LANE)
    return jnp.concatenate([slab(u_tab), slab(v_tab)], axis=1).astype(jnp.bfloat16)


def peer_route(q, sub_keys):
    t = q.shape[0]
    q = q.reshape(t, PEER_HEADS, 2, PEER_QDIM // 2)
    s = jnp.einsum('thpd,pkd->thpk', q, sub_keys, precision=lax.Precision.HIGHEST)
    s1, i1 = lax.top_k(s[:, :, 0], PEER_TOPK)
    s2, i2 = lax.top_k(s[:, :, 1], PEER_TOPK)
    cand = (s1[..., :, None] + s2[..., None, :]).reshape(t, PEER_HEADS, PEER_TOPK * PEER_TOPK)
    cid = (i1[..., :, None] * PEER_NKEYS + i2[..., None, :]).reshape(t, PEER_HEADS, PEER_TOPK * PEER_TOPK)
    sc, j = lax.top_k(cand, PEER_TOPK)
    eid = jnp.take_along_axis(cid, j, axis=-1)
    gate = jax.nn.softmax(sc, axis=-1)
    return eid.reshape(t, N_SEL).astype(jnp.int32), gate.reshape(t, N_SEL)


def peer_layer(y, g, w_q, sub_keys, tab):
    bsz, L, d = y.shape
    yf = y.reshape(bsz * L, d)
    xn = _rmsnorm(yf, g, jnp.float32)
    q = _matmul(xn.astype(jnp.bfloat16), w_q)
    eid, gate = peer_route(q, sub_keys)
    out = peer_gather(yf, xn, eid, gate, tab, min(PEER_TOK_BLOCK, bsz * L))
    return out.reshape(bsz, L, d)


def _norm3(x, g):
    bsz, L, d = x.shape
    return _rmsnorm(x.reshape(bsz * L, d), g, jnp.bfloat16).reshape(bsz, L, d)


def kernel(x_prompt, x_sample, cache_k, cache_v, state_delta, state_conv, page_table,
           norm_mix, norm_ffn, ab_w_in, ab_w_out, a_conv_w, a_log, a_dt_bias, a_out_norm,
           b_q_norm, b_k_norm, c_w_in, c_b_in, c_ln_g, c_ln_b, c_w_s, c_b_s, c_w_out,
           peer_w_q, peer_sub_keys, peer_u, peer_v):
    bp = x_prompt.shape[0]
    bs = x_sample.shape[0]
    past_len = page_table.shape[1] * cache_k.shape[2]
    bf16 = jnp.bfloat16
    yp, ys = x_prompt, x_sample
    kp_l, vp_l, dp_l, cp_l = [], [], [], []
    ks_l, vs_l, ds_l, cs_l = [], [], [], []
    sv_l = []
    for layer in range(DEPTH):
        hp = _norm3(yp, norm_mix[layer])
        hs = _norm3(ys, norm_mix[layer])
        if layer % 2 == 0:
            i = layer // 2
            w_in = ab_w_in[i]
            split = A_CONV_DIM + A_V_W
            w_small = jnp.pad(w_in[:, split:P_A], ((0, 0), (0, 128 - 2 * A_HEADS))).astype(bf16)
            w_in_parts = (w_in[:, :split].astype(bf16), w_small, w_in[:, P_A:].astype(bf16))
            w = (w_in_parts, ab_w_out[i].astype(bf16), a_conv_w[i], a_log[i], a_dt_bias[i], a_out_norm[i],
                 b_q_norm[i], b_k_norm[i])
            conv0 = jnp.zeros((bp, A_CONV - 1, A_CONV_DIM), x_prompt.dtype)
            s0 = jnp.zeros((bp, A_HEADS, A_DK, A_DV), jnp.float32)
            yp, kp, vp, cp, dp = ab_mixer(yp, hp, 0, conv0, s0, None, *w)
            ys, kn, vn, cn, dn = ab_mixer(ys, hs, past_len, state_conv[i], state_delta[i],
                                          (cache_k, cache_v, i, page_table), *w)
            kp_l.append(kp); vp_l.append(vp); dp_l.append(dp); cp_l.append(cp)
            ks_l.append(kn); vs_l.append(vn); ds_l.append(dn); cs_l.append(cn)
        else:
            j = layer // 2
            w = (c_w_in[j].astype(bf16), c_b_in[j], c_ln_g[j], c_ln_b[j], c_w_s[j], c_b_s[j], c_w_out[j].astype(bf16))
            yp, _ = c_mixer(yp, hp, *w)
            ys, sv = c_mixer(ys, hs, *w)
            sv_l.append(sv)
        pw = (peer_w_q[layer].astype(bf16), peer_sub_keys[layer], pack_tables(peer_u[layer], peer_v[layer]))
        yp = peer_layer(yp, norm_ffn[layer], *pw)
        ys = peer_layer(ys, norm_ffn[layer], *pw)
    return (yp, ys, jnp.stack(kp_l), jnp.stack(vp_l), jnp.stack(dp_l), jnp.stack(cp_l),
            jnp.stack(ks_l), jnp.stack(vs_l), jnp.stack(ds_l), jnp.stack(cs_l), jnp.stack(sv_l))
```

```python
import functools
import math

import jax
import jax.numpy as jnp
from jax import lax
from jax.experimental import pallas as pl
from jax.experimental.pallas import tpu as pltpu

D_MODEL = 4096
DEPTH = 2

A_HEADS = 16
A_DK = 128
A_DV = 128
A_CONV = 4
A_CHUNK = 64
A_QK_W = A_HEADS * A_DK
A_V_W = A_HEADS * A_DV
A_CONV_DIM = 2 * A_QK_W + A_V_W
P_A = A_CONV_DIM + A_V_W + 2 * A_HEADS

B_HEADS = 16
B_DH = 128
B_W = B_HEADS * B_DH
MOBA_BLOCK = 256
MOBA_TOPK = 3
ROPE_THETA = 500000.0
ROPE_DIM = B_DH // 4
Q_BLOCK = 128

C_WIDTH = D_MODEL
C_GROUPS = 16
C_GROUP_W = C_WIDTH // C_GROUPS
C_CHUNK = 128

PEER_HEADS = 8
PEER_NKEYS = 128
PEER_QDIM = 256
PEER_TOPK = 16
PEER_TOK_BLOCK = 128

NORM_EPS = 1e-6
LN_EPS = 1e-5

V7X_VMEM_LIMIT_BYTES = 48 * 1024 * 1024


INV_SQRT2 = 0.7071067811865476


def _gelu_exact(x):
    return 0.5 * x * (1.0 + lax.erf(x * INV_SQRT2))


def _mm_kernel(a_ref, b_ref, o_ref):
    o_ref[...] = jnp.dot(a_ref[...], b_ref[...], preferred_element_type=jnp.float32)


def _mm_res_kernel(a_ref, b_ref, r_ref, o_ref):
    o_ref[...] = r_ref[...] + jnp.dot(a_ref[...], b_ref[...], preferred_element_type=jnp.float32)


def _mm_bias_gelu_kernel(a_ref, b_ref, bias_ref, o_ref):
    o_ref[...] = _gelu_exact(jnp.dot(a_ref[...], b_ref[...], preferred_element_type=jnp.float32) + bias_ref[...])


def _matmul(a, b, residual=None, gelu_bias=None):
    m, k = a.shape
    n = b.shape[1]
    tm = min(m, 512)
    tn = min(n, 1024)
    assert m % tm == 0 and n % tn == 0, (m, n)
    assert residual is None or gelu_bias is None
    grid = (n // tn, m // tm)
    in_specs = [pl.BlockSpec((tm, k), lambda j, i: (i, 0)),
                pl.BlockSpec((k, tn), lambda j, i: (0, j))]
    args = [a, b]
    body = _mm_kernel
    if residual is not None:
        in_specs.append(pl.BlockSpec((tm, tn), lambda j, i: (i, j)))
        args.append(residual)
        body = _mm_res_kernel
    if gelu_bias is not None:
        in_specs.append(pl.BlockSpec((1, tn), lambda j, i: (0, j)))
        args.append(gelu_bias.reshape(1, n))
        body = _mm_bias_gelu_kernel
    return pl.pallas_call(
        body,
        grid=grid,
        in_specs=in_specs,
        out_specs=pl.BlockSpec((tm, tn), lambda j, i: (i, j)),
        out_shape=jax.ShapeDtypeStruct((m, n), jnp.float32),
        compiler_params=pltpu.CompilerParams(
            dimension_semantics=("arbitrary", "arbitrary"),
            vmem_limit_bytes=V7X_VMEM_LIMIT_BYTES),
        name="matmul",
    )(*args)


def _rms_kernel(x_ref, g_ref, o_ref):
    x = x_ref[...]
    y = x * lax.rsqrt(jnp.mean(x * x, axis=-1, keepdims=True) + NORM_EPS)
    o_ref[...] = (y * g_ref[...]).astype(o_ref.dtype)


def _rmsnorm(x, g, out_dtype):
    t, d = x.shape
    tr = min(t, 256)
    assert t % tr == 0
    return pl.pallas_call(
        _rms_kernel,
        grid=(t // tr,),
        in_specs=[pl.BlockSpec((tr, d), lambda i: (i, 0)),
                  pl.BlockSpec((1, d), lambda i: (0, 0))],
        out_specs=pl.BlockSpec((tr, d), lambda i: (i, 0)),
        out_shape=jax.ShapeDtypeStruct((t, d), out_dtype),
        compiler_params=pltpu.CompilerParams(dimension_semantics=("arbitrary",)),
        name="rmsnorm",
    )(x, g.reshape(1, d))


def rms_norm(x, g):
    xf = x.astype(jnp.float32)
    y = xf * lax.rsqrt(jnp.mean(xf * xf, axis=-1, keepdims=True) + NORM_EPS)
    return (y * g.astype(jnp.float32)).astype(x.dtype)


def layer_norm(x, g, b):
    xf = x.astype(jnp.float32)
    mu = jnp.mean(xf, axis=-1, keepdims=True)
    xc = xf - mu
    y = xc * lax.rsqrt(jnp.mean(xc * xc, axis=-1, keepdims=True) + LN_EPS)
    return (y * g.astype(jnp.float32) + b.astype(jnp.float32)).astype(x.dtype)


def l2_norm(x):
    xf = x.astype(jnp.float32)
    return xf * lax.rsqrt(jnp.sum(xf * xf, axis=-1, keepdims=True) + NORM_EPS)


def partial_rope(x, pos):
    half = ROPE_DIM // 2
    inv_freq = ROPE_THETA ** (-jnp.arange(half, dtype=jnp.float32) / half)
    ang = pos.astype(jnp.float32)[:, None] * inv_freq[None, :]
    cos = jnp.cos(ang)[:, None, :]
    sin = jnp.sin(ang)[:, None, :]
    xr = x[..., :ROPE_DIM].astype(jnp.float32)
    x1, x2 = xr[..., :half], xr[..., half:]
    rot = jnp.concatenate([x1 * cos - x2 * sin, x2 * cos + x1 * sin], axis=-1)
    return jnp.concatenate([rot.astype(x.dtype), x[..., ROPE_DIM:]], axis=-1)


def gated_delta_chunked(q, k, v, g, beta, s0):
    bsz, L, H, dk = q.shape
    dv = v.shape[-1]
    c = min(A_CHUNK, L)
    n = -(-L // c)
    pad = n * c - L

    def prep(t):
        t = t.astype(jnp.float32)
        t = jnp.pad(t, [(0, 0), (0, pad)] + [(0, 0)] * (t.ndim - 2))
        t = t.reshape((bsz, n, c) + t.shape[2:])
        return jnp.moveaxis(t, (1, 2), (0, 3))

    qc = prep(q) * dk ** -0.5
    kc, vc, gc, bc = prep(k), prep(v), prep(g), prep(beta)
    G = jnp.cumsum(gc, axis=-1)
    tri = jnp.tril(jnp.ones((c, c), bool))
    strict_tri = jnp.tril(jnp.ones((c, c), bool), -1)
    diff = G[..., :, None] - G[..., None, :]
    decay = jnp.where(tri, jnp.exp(jnp.where(tri, diff, 0.0)), 0.0)
    kbeta = kc * bc[..., None]
    lower = jnp.where(strict_tri, jnp.einsum('nbhid,nbhjd->nbhij', kbeta, kc) * decay, 0.0)
    hi = lax.Precision.HIGHEST
    t_inv = jnp.eye(c, dtype=jnp.float32) - lower
    pw2 = lower
    for _ in range(max(c - 1, 0).bit_length() - 1):
        pw2 = jnp.matmul(pw2, pw2, precision=hi)
        t_inv = t_inv + jnp.matmul(t_inv, pw2, precision=hi)
    u = t_inv @ (vc * bc[..., None])
    w = t_inv @ (kbeta * jnp.exp(G)[..., None])
    qk = jnp.where(tri, jnp.einsum('nbhid,nbhjd->nbhij', qc, kc) * decay, 0.0)

    def step(S, xs):
        qi, ki, ui, wi, Gi, qki = xs
        v_new = ui - wi @ S
        o = (qi * jnp.exp(Gi)[..., None]) @ S + qki @ v_new
        g_last = Gi[..., -1]
        S = S * jnp.exp(g_last)[..., None, None] + jnp.einsum(
            'bhcd,bhce->bhde', ki * jnp.exp(g_last[..., None] - Gi)[..., None], v_new)
        return S, o

    S, o = lax.scan(step, s0.astype(jnp.float32), (qc, kc, u, w, G, qk))
    o = jnp.moveaxis(o, (0, 3), (1, 2)).reshape(bsz, n * c, H, dv)[:, :L]
    return o, S.astype(s0.dtype)


DELTA_HEADS_PER_STEP = 4


def _dot(a, b):
    return jnp.dot(a, b, precision=lax.Precision.HIGHEST, preferred_element_type=jnp.float32)


def _dot_nt(a, b):
    return lax.dot_general(a, b, (((1,), (1,)), ((), ())), precision=lax.Precision.HIGHEST,
                           preferred_element_type=jnp.float32)


def _dot_tn(a, b):
    return lax.dot_general(a, b, (((0,), (0,)), ((), ())), precision=lax.Precision.HIGHEST,
                           preferred_element_type=jnp.float32)


def _delta_kernel(q_ref, k_ref, v_ref, g_ref, b_ref, s0_ref, o_ref, s_ref, *, n_chunks, heads):
    c = A_CHUNK
    ri = lax.broadcasted_iota(jnp.int32, (c, c), 0)
    ci = lax.broadcasted_iota(jnp.int32, (c, c), 1)
    tril = ri >= ci
    strict = ri > ci
    eye = ri == ci
    eye_f = eye.astype(jnp.float32)

    def to_col(row):
        return jnp.sum(jnp.where(eye, jnp.broadcast_to(row, (c, c)), 0.0), axis=1, keepdims=True)

    def chunk(n, states):
        r0 = pl.multiple_of(n * c, c)
        hs = range(heads)
        cols = [slice(h * A_DK, (h + 1) * A_DK) for h in hs]
        qc = [q_ref[0, pl.ds(r0, c), cols[h]] * (A_DK ** -0.5) for h in hs]
        kc = [k_ref[0, pl.ds(r0, c), cols[h]] for h in hs]
        vc = [v_ref[0, pl.ds(r0, c), cols[h]] for h in hs]
        g_row = [g_ref[0, h, pl.ds(n, 1), :] for h in hs]
        beta = [to_col(b_ref[0, h, pl.ds(n, 1), :]) for h in hs]
        G_col = [jnp.sum(jnp.where(tril, jnp.broadcast_to(g_row[h], (c, c)), 0.0), axis=1, keepdims=True) for h in hs]
        G_row = [jnp.sum(jnp.where(eye, jnp.broadcast_to(G_col[h], (c, c)), 0.0), axis=0, keepdims=True) for h in hs]
        decay = [jnp.where(tril, jnp.exp(jnp.where(tril, G_col[h] - G_row[h], 0.0)), 0.0) for h in hs]
        kbeta = [kc[h] * beta[h] for h in hs]
        low = [jnp.where(strict, _dot_nt(kbeta[h], kc[h]) * decay[h], 0.0) for h in hs]
        qk = [_dot_nt(qc[h], kc[h]) * decay[h] for h in hs]
        t_inv = [eye_f - low[h] for h in hs]
        pw = low
        for _ in range((c - 1).bit_length() - 1):
            pw = [_dot(pw[h], pw[h]) for h in hs]
            t_inv = [t_inv[h] + _dot(t_inv[h], pw[h]) for h in hs]
        eG = [jnp.exp(G_col[h]) for h in hs]
        u = [_dot(t_inv[h], vc[h] * beta[h]) for h in hs]
        w = [_dot(t_inv[h], kbeta[h] * eG[h]) for h in hs]
        v_new = [u[h] - _dot(w[h], states[h]) for h in hs]
        qs = [_dot(qc[h] * eG[h], states[h]) for h in hs]
        out = [qs[h] + _dot(qk[h], v_new[h]) for h in hs]
        g_last = [jnp.sum(g_row[h], axis=1, keepdims=True) for h in hs]
        new_states = [states[h] * jnp.exp(g_last[h]) + _dot_tn(kc[h] * jnp.exp(g_last[h] - G_col[h]), v_new[h])
                      for h in hs]
        for h in hs:
            o_ref[0, pl.ds(r0, c), cols[h]] = out[h]
        return tuple(new_states)

    init = tuple(s0_ref[0, h] for h in range(heads))
    final = lax.fori_loop(0, n_chunks, chunk, init)
    for h in range(heads):
        s_ref[0, h] = final[h]


def gated_delta(q, k, v, g, beta, s0, heads_per_step):
    bsz, L, w = q.shape
    H = w // A_DK
    assert L % A_CHUNK == 0 and H % heads_per_step == 0
    hb = heads_per_step
    kern = functools.partial(_delta_kernel, n_chunks=L // A_CHUNK, heads=hb)
    seq = pl.BlockSpec((1, L, hb * A_DK), lambda b, h: (b, 0, h))
    gb = pl.BlockSpec((1, hb, L // A_CHUNK, A_CHUNK), lambda b, h: (b * (H // hb) + h, 0, 0, 0))
    st = pl.BlockSpec((1, hb, A_DK, A_DV), lambda b, h: (b, h, 0, 0))
    grp = lambda a: a.reshape(bsz * (H // hb), hb, L // A_CHUNK, A_CHUNK)
    return pl.pallas_call(
        kern,
        grid=(bsz, H // hb),
        in_specs=[seq, seq, seq, gb, gb, st],
        out_specs=[seq, st],
        out_shape=[jax.ShapeDtypeStruct((bsz, L, H * A_DV), jnp.float32),
                   jax.ShapeDtypeStruct((bsz, H, A_DK, A_DV), jnp.float32)],
        compiler_params=pltpu.CompilerParams(
            dimension_semantics=("arbitrary", "arbitrary"),
            vmem_limit_bytes=V7X_VMEM_LIMIT_BYTES),
        name="gated_delta",
    )(q, k, v, grp(g), grp(beta), s0)


def mixer_a(pa, conv_buf, s0, conv_w, a_log, dt_bias, out_norm):
    bsz, L, _ = pa.shape
    qkv = pa[..., :A_CONV_DIM]
    z = pa[..., A_CONV_DIM:A_CONV_DIM + A_V_W]
    b_logit = pa[..., A_CONV_DIM + A_V_W:A_CONV_DIM + A_V_W + A_HEADS]
    a_in = pa[..., A_CONV_DIM + A_V_W + A_HEADS:]
    xp = jnp.concatenate([conv_buf.astype(qkv.dtype), qkv], axis=1)
    conv = jax.nn.silu(sum(xp[:, j:j + L] * conv_w[:, j] for j in range(A_CONV)))
    new_buf = xp[:, L:]
    q = l2_norm(conv[..., :A_QK_W].reshape(bsz, L, A_HEADS, A_DK))
    k = l2_norm(conv[..., A_QK_W:2 * A_QK_W].reshape(bsz, L, A_HEADS, A_DK))
    v = conv[..., 2 * A_QK_W:].reshape(bsz, L, A_HEADS, A_DV)
    beta = jax.nn.sigmoid(b_logit.astype(jnp.float32))
    g = -jnp.exp(a_log.astype(jnp.float32)) * jax.nn.softplus(a_in.astype(jnp.float32) + dt_bias.astype(jnp.float32))
    if L % A_CHUNK == 0:
        flat = lambda a: a.reshape(bsz, L, -1)
        o, s_new = gated_delta(flat(q), flat(k), flat(v), jnp.moveaxis(g, 1, 2), jnp.moveaxis(beta, 1, 2),
                               s0, DELTA_HEADS_PER_STEP)
        o = o.reshape(bsz, L, A_HEADS, A_DV)
    else:
        with jax.default_matmul_precision("highest"):
            o, s_new = gated_delta_chunked(q, k, v, g, beta, s0)
    o = rms_norm(o, out_norm) * jax.nn.silu(z.reshape(bsz, L, A_HEADS, A_DV).astype(jnp.float32))
    return o.reshape(bsz, L, A_V_W).astype(pa.dtype), new_buf, s_new


def _kmean_kernel(pt_ref, *refs, page_size):
    k_refs, o_ref = refs[:-1], refs[-1]
    total = jnp.sum(k_refs[0][0, 0], axis=0)
    for k_ref in k_refs[1:]:
        total = total + jnp.sum(k_ref[0, 0], axis=0)
    o_ref[0, 0] = total * (1.0 / (len(k_refs) * page_size))


def moba_past_kmean(cache_k, layer, page_table):
    _, _, page_size, n_heads, dh = cache_k.shape
    bsz, n_pages = page_table.shape
    ppb = MOBA_BLOCK // page_size
    assert MOBA_BLOCK % page_size == 0 and n_pages % ppb == 0
    kern = functools.partial(_kmean_kernel, page_size=page_size)
    page = lambda j: pl.BlockSpec((1, 1, page_size, n_heads, dh),
                                  lambda b, n, pt: (layer, pt[b, n * ppb + j], 0, 0, 0))
    return pl.pallas_call(
        kern,
        grid_spec=pltpu.PrefetchScalarGridSpec(
            num_scalar_prefetch=1,
            grid=(bsz, n_pages // ppb),
            in_specs=[page(j) for j in range(ppb)],
            out_specs=pl.BlockSpec((1, 1, n_heads, dh), lambda b, n, pt: (b, n, 0, 0))),
        out_shape=jax.ShapeDtypeStruct((bsz, n_pages // ppb, n_heads, dh), jnp.float32),
        compiler_params=pltpu.CompilerParams(dimension_semantics=("arbitrary", "arbitrary")),
        name="moba_past_kmean",
    )(page_table, *([cache_k] * ppb))


def _moba_decode_kernel(pg_ref, q_ref, kn_ref, vn_ref, ck_hbm, cv_hbm, o_ref, kbuf, vbuf, sem, *,
                        layer, n_sel_pages, page_size):
    b = pl.program_id(0)
    h = pl.program_id(1)

    def page_copy(src, dst, s, which):
        return pltpu.make_async_copy(src.at[layer, pg_ref[b, h, s], :, h, :],
                                     dst.at[pl.ds(s * page_size, page_size), :], sem.at[which, s])

    for s in range(n_sel_pages):
        page_copy(ck_hbm, kbuf, s, 0).start()
        page_copy(cv_hbm, vbuf, s, 1).start()

    scale = B_DH ** -0.5
    q = q_ref[0, pl.ds(h, 1), :] * scale
    kn = kn_ref[0, pl.ds(h, 1), :]
    vn = vn_ref[0, pl.ds(h, 1), :]
    s_own = jnp.sum(q * kn, axis=1, keepdims=True)

    for s in range(n_sel_pages):
        page_copy(ck_hbm, kbuf, s, 0).wait()
        page_copy(cv_hbm, vbuf, s, 1).wait()

    sc = lax.dot_general(q, kbuf[...], (((1,), (1,)), ((), ())), precision=lax.Precision.HIGHEST,
                         preferred_element_type=jnp.float32)
    m = jnp.maximum(jnp.max(sc, axis=1, keepdims=True), s_own)
    p = jnp.exp(sc - m)
    p_own = jnp.exp(s_own - m)
    l = jnp.sum(p, axis=1, keepdims=True) + p_own
    acc = jnp.dot(p, vbuf[...], precision=lax.Precision.HIGHEST, preferred_element_type=jnp.float32) + p_own * vn
    o_ref[0, pl.ds(h, 1), :] = acc / l


def moba_decode(q, k_new, v_new, sel_pages, cache_k, cache_v, layer):
    bsz, n_heads, dh = q.shape
    page_size = cache_k.shape[2]
    n_sel_pages = sel_pages.shape[-1]
    kern = functools.partial(_moba_decode_kernel, layer=layer, n_sel_pages=n_sel_pages, page_size=page_size)
    row = pl.BlockSpec((1, n_heads, dh), lambda b, h, pg: (b, 0, 0))
    return pl.pallas_call(
        kern,
        grid_spec=pltpu.PrefetchScalarGridSpec(
            num_scalar_prefetch=1,
            grid=(bsz, n_heads),
            in_specs=[row, row, row, pl.BlockSpec(memory_space=pl.ANY), pl.BlockSpec(memory_space=pl.ANY)],
            out_specs=row,
            scratch_shapes=[pltpu.VMEM((n_sel_pages * page_size, dh), jnp.float32),
                            pltpu.VMEM((n_sel_pages * page_size, dh), jnp.float32),
                            pltpu.SemaphoreType.DMA((2, n_sel_pages))]),
        out_shape=jax.ShapeDtypeStruct((bsz, n_heads, dh), jnp.float32),
        compiler_params=pltpu.CompilerParams(dimension_semantics=("arbitrary", "arbitrary")),
        name="moba_decode",
    )(sel_pages, q, k_new, v_new, cache_k, cache_v)


def moba_sample(q, k_new, v_new, cache_k, cache_v, layer, page_table):
    page_size = cache_k.shape[2]
    ppb = MOBA_BLOCK // page_size
    n_blocks = page_table.shape[1] // ppb
    assert (page_table.shape[1] * page_size) % MOBA_BLOCK == 0 and n_blocks >= MOBA_TOPK
    kmean = moba_past_kmean(cache_k, layer, page_table)
    gate = jnp.einsum('bhd,bnhd->bhn', q, kmean, precision=lax.Precision.HIGHEST)
    _, sel = lax.top_k(gate, MOBA_TOPK)
    blk_pages = page_table.reshape(page_table.shape[0], n_blocks, ppb)
    sel_pages = jnp.take_along_axis(blk_pages[:, None], sel[..., None], axis=2)
    sel_pages = sel_pages.reshape(sel.shape[0], sel.shape[1], MOBA_TOPK * ppb).astype(jnp.int32)
    return moba_decode(q, k_new, v_new, sel_pages, cache_k, cache_v, layer)


def _project(h, w_bf16):
    bsz, L, d = h.shape
    return _matmul(h.reshape(bsz * L, d).astype(jnp.bfloat16), w_bf16).reshape(bsz, L, -1)


def ab_mixer(y, h, pos0, conv_buf, s0, past, w_in, w_out, conv_w, a_log, dt_bias, out_norm, q_norm, k_norm):
    bsz, L, _ = h.shape
    w_main, w_small, w_b = w_in
    p_main = _project(h, w_main)
    p_small = _project(h, w_small)[..., :2 * A_HEADS]
    pb = _project(h, w_b)
    pa = jnp.concatenate([p_main, p_small], axis=-1)
    oa, conv_new, s_new = mixer_a(pa, conv_buf, s0, conv_w, a_log, dt_bias, out_norm)
    pos = pos0 + jnp.arange(L, dtype=jnp.int32)
    q = partial_rope(rms_norm(pb[..., :B_W].reshape(bsz, L, B_HEADS, B_DH), q_norm), pos)
    k = partial_rope(rms_norm(pb[..., B_W:2 * B_W].reshape(bsz, L, B_HEADS, B_DH), k_norm), pos)
    v = pb[..., 2 * B_W:].reshape(bsz, L, B_HEADS, B_DH)
    if past is not None:
        assert L == 1, "the cached path takes one new token per sequence"
        ob = moba_sample(q[:, 0], k[:, 0], v[:, 0], *past).reshape(bsz, L, B_W)
    else:
        assert pos0 == 0
        ob = moba_prompt(q.reshape(bsz, L, B_W), k.reshape(bsz, L, B_W), v.reshape(bsz, L, B_W), B_HEADS)
    cat = jnp.concatenate([oa.astype(jnp.bfloat16), ob.astype(jnp.bfloat16)], axis=-1)
    d = y.shape[-1]
    y = _matmul(cat.reshape(bsz * L, -1), w_out, residual=y.reshape(bsz * L, d)).reshape(bsz, L, d)
    return y, k, v, conv_new, s_new


def _sgu_kernel(u_ref, v_ref, lng_ref, lnb_ref, ws_ref, bst_ref, o_ref):
    v = v_ref[...]
    mu = jnp.mean(v, axis=-1, keepdims=True)
    vc = v - mu
    vn = vc * lax.rsqrt(jnp.mean(vc * vc, axis=-1, keepdims=True) + LN_EPS) * lng_ref[...] + lnb_ref[...]
    c = v.shape[0]
    causal = lax.broadcasted_iota(jnp.int32, (c, c), 0) >= lax.broadcasted_iota(jnp.int32, (c, c), 1)
    for g in range(C_GROUPS):
        cols = slice(g * C_GROUP_W, (g + 1) * C_GROUP_W)
        ws = jnp.where(causal, ws_ref[g], 0.0).astype(jnp.bfloat16)
        mix = jnp.dot(ws, vn[:, cols].astype(jnp.bfloat16), preferred_element_type=jnp.float32)
        o_ref[:, cols] = (u_ref[:, cols] * (mix + bst_ref[:, g:g + 1])).astype(o_ref.dtype)


def _sgu_chunks(z, ln_g, ln_b, w_s, b_s):
    t = z.shape[0]
    assert t % C_CHUNK == 0
    row = lambda a: a.reshape(1, C_WIDTH)
    return pl.pallas_call(
        _sgu_kernel,
        grid=(t // C_CHUNK,),
        in_specs=[pl.BlockSpec((C_CHUNK, C_WIDTH), lambda i: (i, 0)),
                  pl.BlockSpec((C_CHUNK, C_WIDTH), lambda i: (i, 1)),
                  pl.BlockSpec((1, C_WIDTH), lambda i: (0, 0)),
                  pl.BlockSpec((1, C_WIDTH), lambda i: (0, 0)),
                  pl.BlockSpec((C_GROUPS, C_CHUNK, C_CHUNK), lambda i: (0, 0, 0)),
                  pl.BlockSpec((C_CHUNK, C_GROUPS), lambda i: (0, 0))],
        out_specs=pl.BlockSpec((C_CHUNK, C_WIDTH), lambda i: (i, 0)),
        out_shape=jax.ShapeDtypeStruct((t, C_WIDTH), jnp.bfloat16),
        compiler_params=pltpu.CompilerParams(dimension_semantics=("arbitrary",)),
        name="sgu",
    )(z, z, row(ln_g), row(ln_b), w_s, b_s.T)


def c_mixer(y, h, w_in, b_in, ln_g, ln_b, w_s, b_s, w_out):
    bsz, L, d = h.shape
    z = _matmul(h.reshape(bsz * L, d), w_in, gelu_bias=b_in)
    if L % C_CHUNK == 0:
        gated = _sgu_chunks(z, ln_g, ln_b, w_s, b_s)
        v = None
    else:
        z = z.reshape(bsz, L, -1)
        u, v = z[..., :C_WIDTH], z[..., C_WIDTH:]
        v = layer_norm(v, ln_g, ln_b)
        c = min(C_CHUNK, L)
        n = -(-L // c)
        lp = n * c
        vp = jnp.pad(v, ((0, 0), (0, lp - L), (0, 0))).reshape(bsz, n, c, C_GROUPS, C_GROUP_W)
        ws = jnp.tril(w_s[:, :c, :c])
        mix = (jnp.einsum('gij,bnjgc->bnigc', ws, vp, precision=lax.Precision.HIGHEST)
               + b_s[:, :c].T[None, None, :, :, None])
        mix = mix.reshape(bsz, lp, C_WIDTH)[:, :L]
        gated = (u * mix).reshape(bsz * L, C_WIDTH).astype(jnp.bfloat16)
    y = _matmul(gated, w_out, residual=y.reshape(bsz * L, d)).reshape(bsz, L, d)
    return y, v


NEG_INF = float("-inf")


def _moba_prompt_kernel(q_ref, k_ref, v_ref, o_ref, kbf, vbf, kmean, *, n_blocks):
    i = pl.program_id(2)
    blk = MOBA_BLOCK

    @pl.when(i == 0)
    def _():
        k = k_ref[0]
        kbf[...] = k.astype(jnp.bfloat16)
        vbf[...] = v_ref[0].astype(jnp.bfloat16)
        kmean[...] = jnp.mean(k.reshape(n_blocks, blk, B_DH), axis=1)

    own = i
    q = q_ref[0]
    gate = lax.dot_general(q, kmean[...], (((1,), (1,)), ((), ())),
                           precision=lax.Precision.HIGHEST,
                           preferred_element_type=jnp.float32)
    lane = lax.broadcasted_iota(jnp.int32, gate.shape, 1)
    avail = lane < own
    sel = jnp.zeros(gate.shape, jnp.bool_)
    for _ in range(MOBA_TOPK):
        g = jnp.where(avail, gate, NEG_INF)
        mx = jnp.max(g, axis=1, keepdims=True)
        first = jnp.min(jnp.where(avail & (g == mx), lane, n_blocks), axis=1, keepdims=True)
        pick = lane == first
        sel = sel | pick
        avail = avail & jnp.logical_not(pick)
    sel_f = sel.astype(jnp.float32)

    qs = (q * (B_DH ** -0.5)).astype(jnp.bfloat16)
    nt = (((1,), (1,)), ((), ()))

    start = pl.multiple_of(own * blk, blk)
    s = lax.dot_general(qs, kbf[pl.ds(start, blk), :], nt, preferred_element_type=jnp.float32)
    row = lax.broadcasted_iota(jnp.int32, s.shape, 0)
    col = lax.broadcasted_iota(jnp.int32, s.shape, 1)
    s = jnp.where(col <= row, s, NEG_INF)
    m0 = jnp.max(s, axis=1, keepdims=True)
    p = jnp.exp(s - m0)
    l0 = jnp.sum(p, axis=1, keepdims=True)
    acc0 = jnp.dot(p.astype(jnp.bfloat16), vbf[pl.ds(start, blk), :], preferred_element_type=jnp.float32)

    def body(n, carry):
        m, l, acc = carry
        st = pl.multiple_of(n * blk, blk)
        s = lax.dot_general(qs, kbf[pl.ds(st, blk), :], nt, preferred_element_type=jnp.float32)
        sel_n = jnp.sum(jnp.where(lane == n, sel_f, 0.0), axis=1, keepdims=True) > 0.0
        s = jnp.where(sel_n, s, NEG_INF)
        m_new = jnp.maximum(m, jnp.max(s, axis=1, keepdims=True))
        corr = jnp.exp(m - m_new)
        p = jnp.exp(s - m_new)
        l = l * corr + jnp.sum(p, axis=1, keepdims=True)
        acc = acc * corr + jnp.dot(p.astype(jnp.bfloat16), vbf[pl.ds(st, blk), :],
                                   preferred_element_type=jnp.float32)
        return m_new, l, acc

    m, l, acc = lax.fori_loop(0, own, body, (m0, l0, acc0))
    o_ref[0] = (acc / l).astype(o_ref.dtype)


def moba_prompt(q, k, v, n_heads):
    bsz, L, _ = q.shape
    assert L % MOBA_BLOCK == 0
    n_blocks = L // MOBA_BLOCK
    kern = functools.partial(_moba_prompt_kernel, n_blocks=n_blocks)
    return pl.pallas_call(
        kern,
        grid=(bsz, n_heads, n_blocks),
        in_specs=[pl.BlockSpec((1, MOBA_BLOCK, B_DH), lambda b, h, i: (b, i, h)),
                  pl.BlockSpec((1, L, B_DH), lambda b, h, i: (b, 0, h)),
                  pl.BlockSpec((1, L, B_DH), lambda b, h, i: (b, 0, h))],
        out_specs=pl.BlockSpec((1, MOBA_BLOCK, B_DH), lambda b, h, i: (b, i, h)),
        out_shape=jax.ShapeDtypeStruct(q.shape, jnp.bfloat16),
        scratch_shapes=[pltpu.VMEM((L, B_DH), jnp.bfloat16),
                        pltpu.VMEM((L, B_DH), jnp.bfloat16),
                        pltpu.VMEM((n_blocks, B_DH), jnp.float32)],
        compiler_params=pltpu.CompilerParams(
            dimension_semantics=("arbitrary", "arbitrary", "arbitrary")),
        name="moba_prompt",
    )(q, k, v)


N_SEL = PEER_HEADS * PEER_TOPK
SUB = 8
LANE = 128
PEER_AHEAD = 2
PEER_SLOTS = PEER_AHEAD + 1


def _peer_gather_kernel(eid_ref, res_ref, x_ref, g_ref, tab_hbm, o_ref,
                        buf, wbuf, pbuf, cbuf, sem, *, tb, rows):
    n_vreg = rows // SUB
    half = N_SEL // 2

    def start_rows(tok, slot, k0, k1):
        for k in range(k0, k1):
            pltpu.make_async_copy(tab_hbm.at[eid_ref[tok, k]], buf.at[slot, k], sem.at[slot]).start()

    def wait_rows(slot):
        pltpu.make_async_copy(tab_hbm.at[pl.ds(0, N_SEL)], wbuf, sem.at[slot]).wait()

    for ahead in range(PEER_AHEAD):
        start_rows(min(ahead, tb - 1), ahead, 0, N_SEL)

    eye = (lax.broadcasted_iota(jnp.int32, (N_SEL, LANE), 0)
           == lax.broadcasted_iota(jnp.int32, (N_SEL, LANE), 1))

    def token(t, c):
        slot = t % PEER_SLOTS
        nslot = (t + PEER_AHEAD) % PEER_SLOTS
        nxt = jnp.minimum(t + PEER_AHEAD, tb - 1)
        x = x_ref[pl.ds(t, 1), :].reshape(rows, LANE)
        xs = [x[i * SUB:(i + 1) * SUB, :] for i in range(n_vreg)]

        wait_rows(slot)

        for k in range(N_SEL):
            if k % SUB == 0:
                start_rows(nxt, nslot, k // 2, k // 2 + SUB // 2)
            u = buf[slot, k, 0:rows, :].astype(jnp.float32)
            part = u[0:SUB, :] * xs[0]
            for i in range(1, n_vreg):
                part = part + u[i * SUB:(i + 1) * SUB, :] * xs[i]
            pbuf[pl.ds(k, 1), :] = jnp.sum(part, axis=0, keepdims=True)

        act = _gelu_exact(jnp.sum(pbuf[...], axis=1, keepdims=True))
        g_row = jnp.broadcast_to(g_ref[pl.ds(t, 1), :], (N_SEL, LANE))
        g_col = jnp.sum(jnp.where(eye, g_row, 0.0), axis=1, keepdims=True)
        cbuf[...] = jnp.broadcast_to(g_col * act, (N_SEL, LANE))

        acc = [jnp.zeros((SUB, LANE), jnp.float32)] * n_vreg
        for k in range(N_SEL):
            if k % SUB == 0:
                start_rows(nxt, nslot, half + k // 2, half + k // 2 + SUB // 2)
            v = buf[slot, k, rows:2 * rows, :].astype(jnp.float32)
            coef = jnp.broadcast_to(cbuf[pl.ds(k, 1), :], (SUB, LANE))
            acc = [acc[i] + v[i * SUB:(i + 1) * SUB, :] * coef for i in range(n_vreg)]
        o_ref[pl.ds(t, 1), :] = (res_ref[pl.ds(t, 1), :]
                                 + jnp.concatenate(acc, axis=0).reshape(1, rows * LANE))
        return c
    lax.fori_loop(0, tb, token, 0)

    for ahead in range(PEER_AHEAD):
        wait_rows((tb + ahead) % PEER_SLOTS)


def peer_gather(res, xn, eid, gate, tab, tb):
    t, d = xn.shape
    rows = d // LANE
    assert t % tb == 0 and d % (LANE * 16) == 0 and tab.shape[1:] == (2 * rows, LANE)
    kern = functools.partial(_peer_gather_kernel, tb=tb, rows=rows)
    blk = pl.BlockSpec((tb, d), lambda i: (i, 0))
    return pl.pallas_call(
        kern,
        grid=(t // tb,),
        in_specs=[pl.BlockSpec((tb, N_SEL), lambda i: (i, 0), memory_space=pltpu.SMEM),
                  blk, blk,
                  pl.BlockSpec((tb, N_SEL), lambda i: (i, 0)),
                  pl.BlockSpec(memory_space=pl.ANY)],
        out_specs=blk,
        out_shape=jax.ShapeDtypeStruct((t, d), jnp.float32),
        scratch_shapes=[pltpu.VMEM((PEER_SLOTS, N_SEL, 2 * rows, LANE), tab.dtype),
                        pltpu.VMEM((N_SEL, 2 * rows, LANE), tab.dtype),
                        pltpu.VMEM((N_SEL, LANE), jnp.float32),
                        pltpu.VMEM((N_SEL, LANE), jnp.float32),
                        pltpu.SemaphoreType.DMA((PEER_SLOTS,))],
        compiler_params=pltpu.CompilerParams(
            dimension_semantics=("arbitrary",),
            vmem_limit_bytes=V7X_VMEM_LIMIT_BYTES),
        name="peer_gather",
    )(eid, res, xn, gate, tab)


def pack_tables(u_tab, v_tab):
    e, d = u_tab.shape
    slab = lambda a: a.reshape(e, d // LANE, LANE)
    return jnp.concatenate([slab(u_tab), slab(v_tab)], axis=1).astype(jnp.bfloat16)


def peer_route(q, sub_keys):
    t = q.shape[0]
    q = q.reshape(t, PEER_HEADS, 2, PEER_QDIM // 2)
    s = jnp.einsum('thpd,pkd->thpk', q, sub_keys, precision=lax.Precision.HIGHEST)
    s1, i1 = lax.top_k(s[:, :, 0], PEER_TOPK)
    s2, i2 = lax.top_k(s[:, :, 1], PEER_TOPK)
    pairs = [(a, b) for a in range(PEER_TOPK) for b in range(PEER_TOPK) if (a + 1) * (b + 1) <= PEER_TOPK]
    lanes = jnp.arange(LANE)
    e1 = jnp.zeros((PEER_TOPK, LANE), jnp.float32).at[jnp.array([a for a, _ in pairs]), lanes[:len(pairs)]].set(1.0)
    e2 = jnp.zeros((PEER_TOPK, LANE), jnp.float32).at[jnp.array([b for _, b in pairs]), lanes[:len(pairs)]].set(1.0)
    spread = lambda x, e: jnp.einsum('thk,kc->thc', x, e, precision=lax.Precision.HIGHEST)
    cand = spread(s1, e1) + spread(s2, e2) + jnp.where(lanes < len(pairs), 0.0, NEG_INF)
    cid = (spread((i1 * PEER_NKEYS).astype(jnp.float32), e1) + spread(i2.astype(jnp.float32), e2)).astype(jnp.int32)
    sc, j = lax.top_k(cand, PEER_TOPK)
    eid = jnp.take_along_axis(cid, j, axis=-1)
    gate = jax.nn.softmax(sc, axis=-1)
    return eid.reshape(t, N_SEL).astype(jnp.int32), gate.reshape(t, N_SEL)


def peer_layer(y, g, w_q, sub_keys, tab):
    bsz, L, d = y.shape
    yf = y.reshape(bsz * L, d)
    xn = _rmsnorm(yf, g, jnp.float32)
    q = _matmul(xn.astype(jnp.bfloat16), w_q)
    eid, gate = peer_route(q, sub_keys)
    out = peer_gather(yf, xn, eid, gate, tab, min(PEER_TOK_BLOCK, bsz * L))
    return out.reshape(bsz, L, d)


def _norm3(x, g):
    bsz, L, d = x.shape
    return _rmsnorm(x.reshape(bsz * L, d), g, jnp.bfloat16).reshape(bsz, L, d)


def kernel(x_prompt, x_sample, cache_k, cache_v, state_delta, state_conv, page_table,
           norm_mix, norm_ffn, ab_w_in, ab_w_out, a_conv_w, a_log, a_dt_bias, a_out_norm,
           b_q_norm, b_k_norm, c_w_in, c_b_in, c_ln_g, c_ln_b, c_w_s, c_b_s, c_w_out,
           peer_w_q, peer_sub_keys, peer_u, peer_v):
    bp = x_prompt.shape[0]
    bs = x_sample.shape[0]
    past_len = page_table.shape[1] * cache_k.shape[2]
    bf16 = jnp.bfloat16
    yp, ys = x_prompt, x_sample
    kp_l, vp_l, dp_l, cp_l = [], [], [], []
    ks_l, vs_l, ds_l, cs_l = [], [], [], []
    sv_l = []
    for layer in range(DEPTH):
        hp = _norm3(yp, norm_mix[layer])
        hs = _norm3(ys, norm_mix[layer])
        if layer % 2 == 0:
            i = layer // 2
            w_in = ab_w_in[i]
            split = A_CONV_DIM + A_V_W
            w_small = jnp.pad(w_in[:, split:P_A], ((0, 0), (0, 128 - 2 * A_HEADS))).astype(bf16)
            w_in_parts = (w_in[:, :split].astype(bf16), w_small, w_in[:, P_A:].astype(bf16))
            w = (w_in_parts, ab_w_out[i].astype(bf16), a_conv_w[i], a_log[i], a_dt_bias[i], a_out_norm[i],
                 b_q_norm[i], b_k_norm[i])
            conv0 = jnp.zeros((bp, A_CONV - 1, A_CONV_DIM), x_prompt.dtype)
            s0 = jnp.zeros((bp, A_HEADS, A_DK, A_DV), jnp.float32)
            yp, kp, vp, cp, dp = ab_mixer(yp, hp, 0, conv0, s0, None, *w)
            ys, kn, vn, cn, dn = ab_mixer(ys, hs, past_len, state_conv[i], state_delta[i],
                                          (cache_k, cache_v, i, page_table), *w)
            kp_l.append(kp); vp_l.append(vp); dp_l.append(dp); cp_l.append(cp)
            ks_l.append(kn); vs_l.append(vn); ds_l.append(dn); cs_l.append(cn)
        else:
            j = layer // 2
            w = (c_w_in[j].astype(bf16), c_b_in[j], c_ln_g[j], c_ln_b[j], c_w_s[j], c_b_s[j], c_w_out[j].astype(bf16))
            yp, _ = c_mixer(yp, hp, *w)
            ys, sv = c_mixer(ys, hs, *w)
            sv_l.append(sv)
        pw = (peer_w_q[layer].astype(bf16), peer_sub_keys[layer], pack_tables(peer_u[layer], peer_v[layer]))
        yp = peer_layer(yp, norm_ffn[layer], *pw)
        ys = peer_layer(ys, norm_ffn[layer], *pw)
    return (yp, ys, jnp.stack(kp_l), jnp.stack(vp_l), jnp.stack(dp_l), jnp.stack(cp_l),
            jnp.stack(ks_l), jnp.stack(vs_l), jnp.stack(ds_l), jnp.stack(cs_l), jnp.stack(sv_l))
```

```python
import functools
import math

import jax
import jax.numpy as jnp
from jax import lax
from jax.experimental import pallas as pl
from jax.experimental.pallas import tpu as pltpu

D_MODEL = 4096
DEPTH = 2

A_HEADS = 16
A_DK = 128
A_DV = 128
A_CONV = 4
A_CHUNK = 64
A_QK_W = A_HEADS * A_DK
A_V_W = A_HEADS * A_DV
A_CONV_DIM = 2 * A_QK_W + A_V_W
P_A = A_CONV_DIM + A_V_W + 2 * A_HEADS

B_HEADS = 16
B_DH = 128
B_W = B_HEADS * B_DH
MOBA_BLOCK = 256
MOBA_TOPK = 3
ROPE_THETA = 500000.0
ROPE_DIM = B_DH // 4
Q_BLOCK = 128

C_WIDTH = D_MODEL
C_GROUPS = 16
C_GROUP_W = C_WIDTH // C_GROUPS
C_CHUNK = 128

PEER_HEADS = 8
PEER_NKEYS = 128
PEER_QDIM = 256
PEER_TOPK = 16
PEER_TOK_BLOCK = 128

NORM_EPS = 1e-6
LN_EPS = 1e-5

V7X_VMEM_LIMIT_BYTES = 48 * 1024 * 1024


INV_SQRT2 = 0.7071067811865476


def _gelu_exact(x):
    return 0.5 * x * (1.0 + lax.erf(x * INV_SQRT2))


def _mm_kernel(a_ref, b_ref, o_ref):
    o_ref[...] = jnp.dot(a_ref[...], b_ref[...], preferred_element_type=jnp.float32)


def _mm_res_kernel(a_ref, b_ref, r_ref, o_ref):
    o_ref[...] = r_ref[...] + jnp.dot(a_ref[...], b_ref[...], preferred_element_type=jnp.float32)


def _mm_bias_gelu_kernel(a_ref, b_ref, bias_ref, o_ref):
    o_ref[...] = _gelu_exact(jnp.dot(a_ref[...], b_ref[...], preferred_element_type=jnp.float32) + bias_ref[...])


def _matmul(a, b, residual=None, gelu_bias=None):
    m, k = a.shape
    n = b.shape[1]
    tm = min(m, 512)
    tn = min(n, 1024)
    assert m % tm == 0 and n % tn == 0, (m, n)
    assert residual is None or gelu_bias is None
    grid = (n // tn, m // tm)
    in_specs = [pl.BlockSpec((tm, k), lambda j, i: (i, 0)),
                pl.BlockSpec((k, tn), lambda j, i: (0, j))]
    args = [a, b]
    body = _mm_kernel
    if residual is not None:
        in_specs.append(pl.BlockSpec((tm, tn), lambda j, i: (i, j)))
        args.append(residual)
        body = _mm_res_kernel
    if gelu_bias is not None:
        in_specs.append(pl.BlockSpec((1, tn), lambda j, i: (0, j)))
        args.append(gelu_bias.reshape(1, n))
        body = _mm_bias_gelu_kernel
    return pl.pallas_call(
        body,
        grid=grid,
        in_specs=in_specs,
        out_specs=pl.BlockSpec((tm, tn), lambda j, i: (i, j)),
        out_shape=jax.ShapeDtypeStruct((m, n), jnp.float32),
        compiler_params=pltpu.CompilerParams(
            dimension_semantics=("arbitrary", "arbitrary"),
            vmem_limit_bytes=V7X_VMEM_LIMIT_BYTES),
        name="matmul",
    )(*args)


def _rms_kernel(x_ref, g_ref, o_ref):
    x = x_ref[...]
    y = x * lax.rsqrt(jnp.mean(x * x, axis=-1, keepdims=True) + NORM_EPS)
    o_ref[...] = (y * g_ref[...]).astype(o_ref.dtype)


def _rmsnorm(x, g, out_dtype):
    t, d = x.shape
    tr = min(t, 256)
    assert t % tr == 0
    return pl.pallas_call(
        _rms_kernel,
        grid=(t // tr,),
        in_specs=[pl.BlockSpec((tr, d), lambda i: (i, 0)),
                  pl.BlockSpec((1, d), lambda i: (0, 0))],
        out_specs=pl.BlockSpec((tr, d), lambda i: (i, 0)),
        out_shape=jax.ShapeDtypeStruct((t, d), out_dtype),
        compiler_params=pltpu.CompilerParams(dimension_semantics=("arbitrary",)),
        name="rmsnorm",
    )(x, g.reshape(1, d))


def rms_norm(x, g):
    xf = x.astype(jnp.float32)
    y = xf * lax.rsqrt(jnp.mean(xf * xf, axis=-1, keepdims=True) + NORM_EPS)
    return (y * g.astype(jnp.float32)).astype(x.dtype)


def layer_norm(x, g, b):
    xf = x.astype(jnp.float32)
    mu = jnp.mean(xf, axis=-1, keepdims=True)
    xc = xf - mu
    y = xc * lax.rsqrt(jnp.mean(xc * xc, axis=-1, keepdims=True) + LN_EPS)
    return (y * g.astype(jnp.float32) + b.astype(jnp.float32)).astype(x.dtype)


def l2_norm(x):
    xf = x.astype(jnp.float32)
    return xf * lax.rsqrt(jnp.sum(xf * xf, axis=-1, keepdims=True) + NORM_EPS)


def partial_rope(x, pos):
    half = ROPE_DIM // 2
    inv_freq = ROPE_THETA ** (-jnp.arange(half, dtype=jnp.float32) / half)
    ang = pos.astype(jnp.float32)[:, None] * inv_freq[None, :]
    cos = jnp.cos(ang)[:, None, :]
    sin = jnp.sin(ang)[:, None, :]
    xr = x[..., :ROPE_DIM].astype(jnp.float32)
    x1, x2 = xr[..., :half], xr[..., half:]
    rot = jnp.concatenate([x1 * cos - x2 * sin, x2 * cos + x1 * sin], axis=-1)
    return jnp.concatenate([rot.astype(x.dtype), x[..., ROPE_DIM:]], axis=-1)


def gated_delta_chunked(q, k, v, g, beta, s0):
    bsz, L, H, dk = q.shape
    dv = v.shape[-1]
    c = min(A_CHUNK, L)
    n = -(-L // c)
    pad = n * c - L

    def prep(t):
        t = t.astype(jnp.float32)
        t = jnp.pad(t, [(0, 0), (0, pad)] + [(0, 0)] * (t.ndim - 2))
        t = t.reshape((bsz, n, c) + t.shape[2:])
        return jnp.moveaxis(t, (1, 2), (0, 3))

    qc = prep(q) * dk ** -0.5
    kc, vc, gc, bc = prep(k), prep(v), prep(g), prep(beta)
    G = jnp.cumsum(gc, axis=-1)
    tri = jnp.tril(jnp.ones((c, c), bool))
    strict_tri = jnp.tril(jnp.ones((c, c), bool), -1)
    diff = G[..., :, None] - G[..., None, :]
    decay = jnp.where(tri, jnp.exp(jnp.where(tri, diff, 0.0)), 0.0)
    kbeta = kc * bc[..., None]
    lower = jnp.where(strict_tri, jnp.einsum('nbhid,nbhjd->nbhij', kbeta, kc) * decay, 0.0)
    hi = lax.Precision.HIGHEST
    t_inv = jnp.eye(c, dtype=jnp.float32) - lower
    pw2 = lower
    for _ in range(max(c - 1, 0).bit_length() - 1):
        pw2 = jnp.matmul(pw2, pw2, precision=hi)
        t_inv = t_inv + jnp.matmul(t_inv, pw2, precision=hi)
    u = t_inv @ (vc * bc[..., None])
    w = t_inv @ (kbeta * jnp.exp(G)[..., None])
    qk = jnp.where(tri, jnp.einsum('nbhid,nbhjd->nbhij', qc, kc) * decay, 0.0)

    def step(S, xs):
        qi, ki, ui, wi, Gi, qki = xs
        v_new = ui - wi @ S
        o = (qi * jnp.exp(Gi)[..., None]) @ S + qki @ v_new
        g_last = Gi[..., -1]
        S = S * jnp.exp(g_last)[..., None, None] + jnp.einsum(
            'bhcd,bhce->bhde', ki * jnp.exp(g_last[..., None] - Gi)[..., None], v_new)
        return S, o

    S, o = lax.scan(step, s0.astype(jnp.float32), (qc, kc, u, w, G, qk))
    o = jnp.moveaxis(o, (0, 3), (1, 2)).reshape(bsz, n * c, H, dv)[:, :L]
    return o, S.astype(s0.dtype)


DELTA_HEADS_PER_STEP = 4


def _dot(a, b):
    return jnp.dot(a, b, precision=lax.Precision.HIGHEST, preferred_element_type=jnp.float32)


def _dot_nt(a, b):
    return lax.dot_general(a, b, (((1,), (1,)), ((), ())), precision=lax.Precision.HIGHEST,
                           preferred_element_type=jnp.float32)


def _dot_tn(a, b):
    return lax.dot_general(a, b, (((0,), (0,)), ((), ())), precision=lax.Precision.HIGHEST,
                           preferred_element_type=jnp.float32)


def _delta_kernel(q_ref, k_ref, v_ref, g_ref, b_ref, s0_ref, o_ref, s_ref, *, n_chunks, heads):
    c = A_CHUNK
    ri = lax.broadcasted_iota(jnp.int32, (c, c), 0)
    ci = lax.broadcasted_iota(jnp.int32, (c, c), 1)
    tril = ri >= ci
    strict = ri > ci
    eye = ri == ci
    eye_f = eye.astype(jnp.float32)

    def to_col(row):
        return jnp.sum(jnp.where(eye, jnp.broadcast_to(row, (c, c)), 0.0), axis=1, keepdims=True)

    def chunk(n, states):
        r0 = pl.multiple_of(n * c, c)
        hs = range(heads)
        cols = [slice(h * A_DK, (h + 1) * A_DK) for h in hs]
        qc = [q_ref[0, pl.ds(r0, c), cols[h]] * (A_DK ** -0.5) for h in hs]
        kc = [k_ref[0, pl.ds(r0, c), cols[h]] for h in hs]
        vc = [v_ref[0, pl.ds(r0, c), cols[h]] for h in hs]
        g_row = [g_ref[0, h, pl.ds(n, 1), :] for h in hs]
        beta = [to_col(b_ref[0, h, pl.ds(n, 1), :]) for h in hs]
        G_col = [jnp.sum(jnp.where(tril, jnp.broadcast_to(g_row[h], (c, c)), 0.0), axis=1, keepdims=True) for h in hs]
        G_row = [jnp.sum(jnp.where(eye, jnp.broadcast_to(G_col[h], (c, c)), 0.0), axis=0, keepdims=True) for h in hs]
        decay = [jnp.where(tril, jnp.exp(jnp.where(tril, G_col[h] - G_row[h], 0.0)), 0.0) for h in hs]
        kbeta = [kc[h] * beta[h] for h in hs]
        low = [jnp.where(strict, _dot_nt(kbeta[h], kc[h]) * decay[h], 0.0) for h in hs]
        qk = [_dot_nt(qc[h], kc[h]) * decay[h] for h in hs]
        t_inv = [eye_f - low[h] for h in hs]
        pw = low
        for _ in range((c - 1).bit_length() - 1):
            pw = [_dot(pw[h], pw[h]) for h in hs]
            t_inv = [t_inv[h] + _dot(t_inv[h], pw[h]) for h in hs]
        eG = [jnp.exp(G_col[h]) for h in hs]
        u = [_dot(t_inv[h], vc[h] * beta[h]) for h in hs]
        w = [_dot(t_inv[h], kbeta[h] * eG[h]) for h in hs]
        v_new = [u[h] - _dot(w[h], states[h]) for h in hs]
        qs = [_dot(qc[h] * eG[h], states[h]) for h in hs]
        out = [qs[h] + _dot(qk[h], v_new[h]) for h in hs]
        g_last = [jnp.sum(g_row[h], axis=1, keepdims=True) for h in hs]
        new_states = [states[h] * jnp.exp(g_last[h]) + _dot_tn(kc[h] * jnp.exp(g_last[h] - G_col[h]), v_new[h])
                      for h in hs]
        for h in hs:
            o_ref[0, pl.ds(r0, c), cols[h]] = out[h]
        return tuple(new_states)

    init = tuple(s0_ref[0, h] for h in range(heads))
    final = lax.fori_loop(0, n_chunks, chunk, init)
    for h in range(heads):
        s_ref[0, h] = final[h]


def gated_delta(q, k, v, g, beta, s0, heads_per_step):
    bsz, L, w = q.shape
    H = w // A_DK
    assert L % A_CHUNK == 0 and H % heads_per_step == 0
    hb = heads_per_step
    kern = functools.partial(_delta_kernel, n_chunks=L // A_CHUNK, heads=hb)
    seq = pl.BlockSpec((1, L, hb * A_DK), lambda b, h: (b, 0, h))
    gb = pl.BlockSpec((1, hb, L // A_CHUNK, A_CHUNK), lambda b, h: (b * (H // hb) + h, 0, 0, 0))
    st = pl.BlockSpec((1, hb, A_DK, A_DV), lambda b, h: (b, h, 0, 0))
    grp = lambda a: a.reshape(bsz * (H // hb), hb, L // A_CHUNK, A_CHUNK)
    return pl.pallas_call(
        kern,
        grid=(bsz, H // hb),
        in_specs=[seq, seq, seq, gb, gb, st],
        out_specs=[seq, st],
        out_shape=[jax.ShapeDtypeStruct((bsz, L, H * A_DV), jnp.float32),
                   jax.ShapeDtypeStruct((bsz, H, A_DK, A_DV), jnp.float32)],
        compiler_params=pltpu.CompilerParams(
            dimension_semantics=("arbitrary", "arbitrary"),
            vmem_limit_bytes=V7X_VMEM_LIMIT_BYTES),
        name="gated_delta",
    )(q, k, v, grp(g), grp(beta), s0)


def mixer_a(pa, conv_buf, s0, conv_w, a_log, dt_bias, out_norm):
    bsz, L, _ = pa.shape
    qkv = pa[..., :A_CONV_DIM]
    z = pa[..., A_CONV_DIM:A_CONV_DIM + A_V_W]
    b_logit = pa[..., A_CONV_DIM + A_V_W:A_CONV_DIM + A_V_W + A_HEADS]
    a_in = pa[..., A_CONV_DIM + A_V_W + A_HEADS:]
    xp = jnp.concatenate([conv_buf.astype(qkv.dtype), qkv], axis=1)
    conv = jax.nn.silu(sum(xp[:, j:j + L] * conv_w[:, j] for j in range(A_CONV)))
    new_buf = xp[:, L:]
    q = l2_norm(conv[..., :A_QK_W].reshape(bsz, L, A_HEADS, A_DK))
    k = l2_norm(conv[..., A_QK_W:2 * A_QK_W].reshape(bsz, L, A_HEADS, A_DK))
    v = conv[..., 2 * A_QK_W:].reshape(bsz, L, A_HEADS, A_DV)
    beta = jax.nn.sigmoid(b_logit.astype(jnp.float32))
    g = -jnp.exp(a_log.astype(jnp.float32)) * jax.nn.softplus(a_in.astype(jnp.float32) + dt_bias.astype(jnp.float32))
    if L % A_CHUNK == 0:
        flat = lambda a: a.reshape(bsz, L, -1)
        o, s_new = gated_delta(flat(q), flat(k), flat(v), jnp.moveaxis(g, 1, 2), jnp.moveaxis(beta, 1, 2),
                               s0, DELTA_HEADS_PER_STEP)
        o = o.reshape(bsz, L, A_HEADS, A_DV)
    else:
        with jax.default_matmul_precision("highest"):
            o, s_new = gated_delta_chunked(q, k, v, g, beta, s0)
    o = rms_norm(o, out_norm) * jax.nn.silu(z.reshape(bsz, L, A_HEADS, A_DV).astype(jnp.float32))
    return o.reshape(bsz, L, A_V_W).astype(pa.dtype), new_buf, s_new


def _kmean_kernel(pt_ref, *refs, page_size):
    k_refs, o_ref = refs[:-1], refs[-1]
    total = jnp.sum(k_refs[0][0, 0], axis=0)
    for k_ref in k_refs[1:]:
        total = total + jnp.sum(k_ref[0, 0], axis=0)
    o_ref[0, 0] = total * (1.0 / (len(k_refs) * page_size))


def moba_past_kmean(cache_k, layer, page_table):
    _, _, page_size, n_heads, dh = cache_k.shape
    bsz, n_pages = page_table.shape
    ppb = MOBA_BLOCK // page_size
    assert MOBA_BLOCK % page_size == 0 and n_pages % ppb == 0
    kern = functools.partial(_kmean_kernel, page_size=page_size)
    page = lambda j: pl.BlockSpec((1, 1, page_size, n_heads, dh),
                                  lambda b, n, pt: (layer, pt[b, n * ppb + j], 0, 0, 0))
    return pl.pallas_call(
        kern,
        grid_spec=pltpu.PrefetchScalarGridSpec(
            num_scalar_prefetch=1,
            grid=(bsz, n_pages // ppb),
            in_specs=[page(j) for j in range(ppb)],
            out_specs=pl.BlockSpec((1, 1, n_heads, dh), lambda b, n, pt: (b, n, 0, 0))),
        out_shape=jax.ShapeDtypeStruct((bsz, n_pages // ppb, n_heads, dh), jnp.float32),
        compiler_params=pltpu.CompilerParams(dimension_semantics=("arbitrary", "arbitrary")),
        name="moba_past_kmean",
    )(page_table, *([cache_k] * ppb))


def _moba_decode_kernel(pg_ref, q_ref, kn_ref, vn_ref, ck_hbm, cv_hbm, o_ref, kbuf, vbuf, sem, *,
                        layer, n_sel_pages, page_size):
    b = pl.program_id(0)
    h = pl.program_id(1)

    def page_copy(src, dst, s, which):
        return pltpu.make_async_copy(src.at[layer, pg_ref[b, h, s], :, h, :],
                                     dst.at[pl.ds(s * page_size, page_size), :], sem.at[which, s])

    for s in range(n_sel_pages):
        page_copy(ck_hbm, kbuf, s, 0).start()
        page_copy(cv_hbm, vbuf, s, 1).start()

    scale = B_DH ** -0.5
    q = q_ref[0, pl.ds(h, 1), :] * scale
    kn = kn_ref[0, pl.ds(h, 1), :]
    vn = vn_ref[0, pl.ds(h, 1), :]
    s_own = jnp.sum(q * kn, axis=1, keepdims=True)

    for s in range(n_sel_pages):
        page_copy(ck_hbm, kbuf, s, 0).wait()
        page_copy(cv_hbm, vbuf, s, 1).wait()

    sc = lax.dot_general(q, kbuf[...], (((1,), (1,)), ((), ())), precision=lax.Precision.HIGHEST,
                         preferred_element_type=jnp.float32)
    m = jnp.maximum(jnp.max(sc, axis=1, keepdims=True), s_own)
    p = jnp.exp(sc - m)
    p_own = jnp.exp(s_own - m)
    l = jnp.sum(p, axis=1, keepdims=True) + p_own
    acc = jnp.dot(p, vbuf[...], precision=lax.Precision.HIGHEST, preferred_element_type=jnp.float32) + p_own * vn
    o_ref[0, pl.ds(h, 1), :] = acc / l


def moba_decode(q, k_new, v_new, sel_pages, cache_k, cache_v, layer):
    bsz, n_heads, dh = q.shape
    page_size = cache_k.shape[2]
    n_sel_pages = sel_pages.shape[-1]
    kern = functools.partial(_moba_decode_kernel, layer=layer, n_sel_pages=n_sel_pages, page_size=page_size)
    row = pl.BlockSpec((1, n_heads, dh), lambda b, h, pg: (b, 0, 0))
    return pl.pallas_call(
        kern,
        grid_spec=pltpu.PrefetchScalarGridSpec(
            num_scalar_prefetch=1,
            grid=(bsz, n_heads),
            in_specs=[row, row, row, pl.BlockSpec(memory_space=pl.ANY), pl.BlockSpec(memory_space=pl.ANY)],
            out_specs=row,
            scratch_shapes=[pltpu.VMEM((n_sel_pages * page_size, dh), jnp.float32),
                            pltpu.VMEM((n_sel_pages * page_size, dh), jnp.float32),
                            pltpu.SemaphoreType.DMA((2, n_sel_pages))]),
        out_shape=jax.ShapeDtypeStruct((bsz, n_heads, dh), jnp.float32),
        compiler_params=pltpu.CompilerParams(dimension_semantics=("arbitrary", "arbitrary")),
        name="moba_decode",
    )(sel_pages, q, k_new, v_new, cache_k, cache_v)


def moba_sample(q, k_new, v_new, cache_k, cache_v, layer, page_table):
    page_size = cache_k.shape[2]
    ppb = MOBA_BLOCK // page_size
    n_blocks = page_table.shape[1] // ppb
    assert (page_table.shape[1] * page_size) % MOBA_BLOCK == 0 and n_blocks >= MOBA_TOPK
    kmean = moba_past_kmean(cache_k, layer, page_table)
    gate = jnp.einsum('bhd,bnhd->bhn', q, kmean, precision=lax.Precision.HIGHEST)
    _, sel = lax.top_k(gate, MOBA_TOPK)
    blk_pages = page_table.reshape(page_table.shape[0], n_blocks, ppb)
    sel_pages = jnp.take_along_axis(blk_pages[:, None], sel[..., None], axis=2)
    sel_pages = sel_pages.reshape(sel.shape[0], sel.shape[1], MOBA_TOPK * ppb).astype(jnp.int32)
    return moba_decode(q, k_new, v_new, sel_pages, cache_k, cache_v, layer)


def _project(h, w_bf16):
    bsz, L, d = h.shape
    return _matmul(h.reshape(bsz * L, d).astype(jnp.bfloat16), w_bf16).reshape(bsz, L, -1)


def ab_mixer(y, h, pos0, conv_buf, s0, past, w_in, w_out, conv_w, a_log, dt_bias, out_norm, q_norm, k_norm):
    bsz, L, _ = h.shape
    w_main, w_small, w_b = w_in
    p_main = _project(h, w_main)
    p_small = _project(h, w_small)[..., :2 * A_HEADS]
    pb = _project(h, w_b)
    pa = jnp.concatenate([p_main, p_small], axis=-1)
    oa, conv_new, s_new = mixer_a(pa, conv_buf, s0, conv_w, a_log, dt_bias, out_norm)
    pos = pos0 + jnp.arange(L, dtype=jnp.int32)
    q = partial_rope(rms_norm(pb[..., :B_W].reshape(bsz, L, B_HEADS, B_DH), q_norm), pos)
    k = partial_rope(rms_norm(pb[..., B_W:2 * B_W].reshape(bsz, L, B_HEADS, B_DH), k_norm), pos)
    v = pb[..., 2 * B_W:].reshape(bsz, L, B_HEADS, B_DH)
    if past is not None:
        assert L == 1, "the cached path takes one new token per sequence"
        ob = moba_sample(q[:, 0], k[:, 0], v[:, 0], *past).reshape(bsz, L, B_W)
    else:
        assert pos0 == 0
        ob = moba_prompt(q.reshape(bsz, L, B_W), k.reshape(bsz, L, B_W), v.reshape(bsz, L, B_W), B_HEADS)
    cat = jnp.concatenate([oa.astype(jnp.bfloat16), ob.astype(jnp.bfloat16)], axis=-1)
    d = y.shape[-1]
    y = _matmul(cat.reshape(bsz * L, -1), w_out, residual=y.reshape(bsz * L, d)).reshape(bsz, L, d)
    return y, k, v, conv_new, s_new


def _sgu_kernel(u_ref, v_ref, lng_ref, lnb_ref, ws_ref, bst_ref, o_ref):
    v = v_ref[...]
    mu = jnp.mean(v, axis=-1, keepdims=True)
    vc = v - mu
    vn = vc * lax.rsqrt(jnp.mean(vc * vc, axis=-1, keepdims=True) + LN_EPS) * lng_ref[...] + lnb_ref[...]
    c = v.shape[0]
    causal = lax.broadcasted_iota(jnp.int32, (c, c), 0) >= lax.broadcasted_iota(jnp.int32, (c, c), 1)
    for g in range(C_GROUPS):
        cols = slice(g * C_GROUP_W, (g + 1) * C_GROUP_W)
        ws = jnp.where(causal, ws_ref[g], 0.0).astype(jnp.bfloat16)
        mix = jnp.dot(ws, vn[:, cols].astype(jnp.bfloat16), preferred_element_type=jnp.float32)
        o_ref[:, cols] = (u_ref[:, cols] * (mix + bst_ref[:, g:g + 1])).astype(o_ref.dtype)


def _sgu_chunks(z, ln_g, ln_b, w_s, b_s):
    t = z.shape[0]
    assert t % C_CHUNK == 0
    row = lambda a: a.reshape(1, C_WIDTH)
    return pl.pallas_call(
        _sgu_kernel,
        grid=(t // C_CHUNK,),
        in_specs=[pl.BlockSpec((C_CHUNK, C_WIDTH), lambda i: (i, 0)),
                  pl.BlockSpec((C_CHUNK, C_WIDTH), lambda i: (i, 1)),
                  pl.BlockSpec((1, C_WIDTH), lambda i: (0, 0)),
                  pl.BlockSpec((1, C_WIDTH), lambda i: (0, 0)),
                  pl.BlockSpec((C_GROUPS, C_CHUNK, C_CHUNK), lambda i: (0, 0, 0)),
                  pl.BlockSpec((C_CHUNK, C_GROUPS), lambda i: (0, 0))],
        out_specs=pl.BlockSpec((C_CHUNK, C_WIDTH), lambda i: (i, 0)),
        out_shape=jax.ShapeDtypeStruct((t, C_WIDTH), jnp.bfloat16),
        compiler_params=pltpu.CompilerParams(dimension_semantics=("arbitrary",)),
        name="sgu",
    )(z, z, row(ln_g), row(ln_b), w_s, b_s.T)


def c_mixer(y, h, w_in, b_in, ln_g, ln_b, w_s, b_s, w_out):
    bsz, L, d = h.shape
    z = _matmul(h.reshape(bsz * L, d), w_in, gelu_bias=b_in)
    if L % C_CHUNK == 0:
        gated = _sgu_chunks(z, ln_g, ln_b, w_s, b_s)
        v = None
    else:
        z = z.reshape(bsz, L, -1)
        u, v = z[..., :C_WIDTH], z[..., C_WIDTH:]
        v = layer_norm(v, ln_g, ln_b)
        c = min(C_CHUNK, L)
        n = -(-L // c)
        lp = n * c
        vp = jnp.pad(v, ((0, 0), (0, lp - L), (0, 0))).reshape(bsz, n, c, C_GROUPS, C_GROUP_W)
        ws = jnp.tril(w_s[:, :c, :c])
        mix = (jnp.einsum('gij,bnjgc->bnigc', ws, vp, precision=lax.Precision.HIGHEST)
               + b_s[:, :c].T[None, None, :, :, None])
        mix = mix.reshape(bsz, lp, C_WIDTH)[:, :L]
        gated = (u * mix).reshape(bsz * L, C_WIDTH).astype(jnp.bfloat16)
    y = _matmul(gated, w_out, residual=y.reshape(bsz * L, d)).reshape(bsz, L, d)
    return y, v


NEG_INF = float("-inf")


def _moba_prompt_kernel(q_ref, k_ref, v_ref, o_ref, kbf, vbf, kmean, *, n_blocks):
    i = pl.program_id(2)
    blk = MOBA_BLOCK

    @pl.when(i == 0)
    def _():
        k = k_ref[0]
        kbf[...] = k.astype(jnp.bfloat16)
        vbf[...] = v_ref[0].astype(jnp.bfloat16)
        kmean[...] = jnp.mean(k.reshape(n_blocks, blk, B_DH), axis=1)

    own = i
    q = q_ref[0]
    gate = lax.dot_general(q, kmean[...], (((1,), (1,)), ((), ())),
                           precision=lax.Precision.HIGHEST,
                           preferred_element_type=jnp.float32)
    lane = lax.broadcasted_iota(jnp.int32, gate.shape, 1)
    avail = lane < own
    sel = jnp.zeros(gate.shape, jnp.bool_)
    for _ in range(MOBA_TOPK):
        g = jnp.where(avail, gate, NEG_INF)
        mx = jnp.max(g, axis=1, keepdims=True)
        first = jnp.min(jnp.where(avail & (g == mx), lane, n_blocks), axis=1, keepdims=True)
        pick = lane == first
        sel = sel | pick
        avail = avail & jnp.logical_not(pick)
    sel_f = sel.astype(jnp.float32)

    qs = (q * (B_DH ** -0.5)).astype(jnp.bfloat16)
    nt = (((1,), (1,)), ((), ()))

    start = pl.multiple_of(own * blk, blk)
    s = lax.dot_general(qs, kbf[pl.ds(start, blk), :], nt, preferred_element_type=jnp.float32)
    row = lax.broadcasted_iota(jnp.int32, s.shape, 0)
    col = lax.broadcasted_iota(jnp.int32, s.shape, 1)
    s = jnp.where(col <= row, s, NEG_INF)
    m0 = jnp.max(s, axis=1, keepdims=True)
    p = jnp.exp(s - m0)
    l0 = jnp.sum(p, axis=1, keepdims=True)
    acc0 = jnp.dot(p.astype(jnp.bfloat16), vbf[pl.ds(start, blk), :], preferred_element_type=jnp.float32)

    def body(n, carry):
        m, l, acc = carry
        st = pl.multiple_of(n * blk, blk)
        s = lax.dot_general(qs, kbf[pl.ds(st, blk), :], nt, preferred_element_type=jnp.float32)
        sel_n = jnp.sum(jnp.where(lane == n, sel_f, 0.0), axis=1, keepdims=True) > 0.0
        s = jnp.where(sel_n, s, NEG_INF)
        m_new = jnp.maximum(m, jnp.max(s, axis=1, keepdims=True))
        corr = jnp.exp(m - m_new)
        p = jnp.exp(s - m_new)
        l = l * corr + jnp.sum(p, axis=1, keepdims=True)
        acc = acc * corr + jnp.dot(p.astype(jnp.bfloat16), vbf[pl.ds(st, blk), :],
                                   preferred_element_type=jnp.float32)
        return m_new, l, acc

    m, l, acc = lax.fori_loop(0, own, body, (m0, l0, acc0))
    o_ref[0] = (acc / l).astype(o_ref.dtype)


def moba_prompt(q, k, v, n_heads):
    bsz, L, _ = q.shape
    assert L % MOBA_BLOCK == 0
    n_blocks = L // MOBA_BLOCK
    kern = functools.partial(_moba_prompt_kernel, n_blocks=n_blocks)
    return pl.pallas_call(
        kern,
        grid=(bsz, n_heads, n_blocks),
        in_specs=[pl.BlockSpec((1, MOBA_BLOCK, B_DH), lambda b, h, i: (b, i, h)),
                  pl.BlockSpec((1, L, B_DH), lambda b, h, i: (b, 0, h)),
                  pl.BlockSpec((1, L, B_DH), lambda b, h, i: (b, 0, h))],
        out_specs=pl.BlockSpec((1, MOBA_BLOCK, B_DH), lambda b, h, i: (b, i, h)),
        out_shape=jax.ShapeDtypeStruct(q.shape, jnp.bfloat16),
        scratch_shapes=[pltpu.VMEM((L, B_DH), jnp.bfloat16),
                        pltpu.VMEM((L, B_DH), jnp.bfloat16),
                        pltpu.VMEM((n_blocks, B_DH), jnp.float32)],
        compiler_params=pltpu.CompilerParams(
            dimension_semantics=("arbitrary", "arbitrary", "arbitrary")),
        name="moba_prompt",
    )(q, k, v)


N_SEL = PEER_HEADS * PEER_TOPK
SUB = 8
LANE = 128
PEER_AHEAD = 2
PEER_SLOTS = PEER_AHEAD + 1


def _peer_gather_kernel(eid_ref, res_ref, x_ref, g_ref, tab_hbm, o_ref,
                        buf, wbuf, pbuf, cbuf, sem, *, tb, rows):
    n_vreg = rows // SUB
    half = N_SEL // 2

    def start_rows(tok, slot, k0, k1):
        for k in range(k0, k1):
            pltpu.make_async_copy(tab_hbm.at[eid_ref[tok, k]], buf.at[slot, k], sem.at[slot]).start(priority=k % 2)

    def wait_rows(slot):
        pltpu.make_async_copy(tab_hbm.at[pl.ds(0, N_SEL)], wbuf, sem.at[slot]).wait()

    for ahead in range(PEER_AHEAD):
        start_rows(min(ahead, tb - 1), ahead, 0, N_SEL)

    eye = (lax.broadcasted_iota(jnp.int32, (N_SEL, LANE), 0)
           == lax.broadcasted_iota(jnp.int32, (N_SEL, LANE), 1))

    def token(t, c):
        slot = t % PEER_SLOTS
        nslot = (t + PEER_AHEAD) % PEER_SLOTS
        nxt = jnp.minimum(t + PEER_AHEAD, tb - 1)
        x = x_ref[pl.ds(t, 1), :].reshape(rows, LANE)
        xs = [x[i * SUB:(i + 1) * SUB, :] for i in range(n_vreg)]

        wait_rows(slot)

        for k in range(N_SEL):
            if k % SUB == 0:
                start_rows(nxt, nslot, k // 2, k // 2 + SUB // 2)
            u = buf[slot, k, 0:rows, :].astype(jnp.float32)
            part = u[0:SUB, :] * xs[0]
            for i in range(1, n_vreg):
                part = part + u[i * SUB:(i + 1) * SUB, :] * xs[i]
            pbuf[pl.ds(k, 1), :] = jnp.sum(part, axis=0, keepdims=True)

        act = _gelu_exact(jnp.sum(pbuf[...], axis=1, keepdims=True))
        g_row = jnp.broadcast_to(g_ref[pl.ds(t, 1), :], (N_SEL, LANE))
        g_col = jnp.sum(jnp.where(eye, g_row, 0.0), axis=1, keepdims=True)
        cbuf[...] = jnp.broadcast_to(g_col * act, (N_SEL, LANE))

        acc = [jnp.zeros((SUB, LANE), jnp.float32)] * n_vreg
        for k in range(N_SEL):
            if k % SUB == 0:
                start_rows(nxt, nslot, half + k // 2, half + k // 2 + SUB // 2)
            v = buf[slot, k, rows:2 * rows, :].astype(jnp.float32)
            coef = jnp.broadcast_to(cbuf[pl.ds(k, 1), :], (SUB, LANE))
            acc = [acc[i] + v[i * SUB:(i + 1) * SUB, :] * coef for i in range(n_vreg)]
        o_ref[pl.ds(t, 1), :] = (res_ref[pl.ds(t, 1), :]
                                 + jnp.concatenate(acc, axis=0).reshape(1, rows * LANE))
        return c
    lax.fori_loop(0, tb, token, 0)

    for ahead in range(PEER_AHEAD):
        wait_rows((tb + ahead) % PEER_SLOTS)


def peer_gather(res, xn, eid, gate, tab, tb):
    t, d = xn.shape
    rows = d // LANE
    assert t % tb == 0 and d % (LANE * 16) == 0 and tab.shape[1:] == (2 * rows, LANE)
    kern = functools.partial(_peer_gather_kernel, tb=tb, rows=rows)
    blk = pl.BlockSpec((tb, d), lambda i: (i, 0))
    return pl.pallas_call(
        kern,
        grid=(t // tb,),
        in_specs=[pl.BlockSpec((tb, N_SEL), lambda i: (i, 0), memory_space=pltpu.SMEM),
                  blk, blk,
                  pl.BlockSpec((tb, N_SEL), lambda i: (i, 0)),
                  pl.BlockSpec(memory_space=pl.ANY)],
        out_specs=blk,
        out_shape=jax.ShapeDtypeStruct((t, d), jnp.float32),
        scratch_shapes=[pltpu.VMEM((PEER_SLOTS, N_SEL, 2 * rows, LANE), tab.dtype),
                        pltpu.VMEM((N_SEL, 2 * rows, LANE), tab.dtype),
                        pltpu.VMEM((N_SEL, LANE), jnp.float32),
                        pltpu.VMEM((N_SEL, LANE), jnp.float32),
                        pltpu.SemaphoreType.DMA((PEER_SLOTS,))],
        compiler_params=pltpu.CompilerParams(
            dimension_semantics=("arbitrary",),
            vmem_limit_bytes=V7X_VMEM_LIMIT_BYTES),
        name="peer_gather",
    )(eid, res, xn, gate, tab)


def pack_tables(u_tab, v_tab):
    e, d = u_tab.shape
    slab = lambda a: a.reshape(e, d // LANE, LANE)
    return jnp.concatenate([slab(u_tab), slab(v_tab)], axis=1).astype(jnp.bfloat16)


def peer_route(q, sub_keys):
    t = q.shape[0]
    q = q.reshape(t, PEER_HEADS, 2, PEER_QDIM // 2)
    s = jnp.einsum('thpd,pkd->thpk', q, sub_keys, precision=lax.Precision.HIGHEST)
    s1, i1 = lax.top_k(s[:, :, 0], PEER_TOPK)
    s2, i2 = lax.top_k(s[:, :, 1], PEER_TOPK)
    pairs = [(a, b) for a in range(PEER_TOPK) for b in range(PEER_TOPK) if (a + 1) * (b + 1) <= PEER_TOPK]
    lanes = jnp.arange(LANE)
    e1 = jnp.zeros((PEER_TOPK, LANE), jnp.float32).at[jnp.array([a for a, _ in pairs]), lanes[:len(pairs)]].set(1.0)
    e2 = jnp.zeros((PEER_TOPK, LANE), jnp.float32).at[jnp.array([b for _, b in pairs]), lanes[:len(pairs)]].set(1.0)
    spread = lambda x, e: jnp.einsum('thk,kc->thc', x, e, precision=lax.Precision.HIGHEST)
    cand = spread(s1, e1) + spread(s2, e2) + jnp.where(lanes < len(pairs), 0.0, NEG_INF)
    cid = (spread((i1 * PEER_NKEYS).astype(jnp.float32), e1) + spread(i2.astype(jnp.float32), e2)).astype(jnp.int32)
    sc, j = lax.top_k(cand, PEER_TOPK)
    eid = jnp.take_along_axis(cid, j, axis=-1)
    gate = jax.nn.softmax(sc, axis=-1)
    return eid.reshape(t, N_SEL).astype(jnp.int32), gate.reshape(t, N_SEL)


def peer_layer(y, g, w_q, sub_keys, tab):
    bsz, L, d = y.shape
    yf = y.reshape(bsz * L, d)
    xn = _rmsnorm(yf, g, jnp.float32)
    q = _matmul(xn.astype(jnp.bfloat16), w_q)
    eid, gate = peer_route(q, sub_keys)
    out = peer_gather(yf, xn, eid, gate, tab, min(PEER_TOK_BLOCK, bsz * L))
    return out.reshape(bsz, L, d)


def _norm3(x, g):
    bsz, L, d = x.shape
    return _rmsnorm(x.reshape(bsz * L, d), g, jnp.bfloat16).reshape(bsz, L, d)


def kernel(x_prompt, x_sample, cache_k, cache_v, state_delta, state_conv, page_table,
           norm_mix, norm_ffn, ab_w_in, ab_w_out, a_conv_w, a_log, a_dt_bias, a_out_norm,
           b_q_norm, b_k_norm, c_w_in, c_b_in, c_ln_g, c_ln_b, c_w_s, c_b_s, c_w_out,
           peer_w_q, peer_sub_keys, peer_u, peer_v):
    bp = x_prompt.shape[0]
    bs = x_sample.shape[0]
    past_len = page_table.shape[1] * cache_k.shape[2]
    bf16 = jnp.bfloat16
    yp, ys = x_prompt, x_sample
    kp_l, vp_l, dp_l, cp_l = [], [], [], []
    ks_l, vs_l, ds_l, cs_l = [], [], [], []
    sv_l = []
    for layer in range(DEPTH):
        hp = _norm3(yp, norm_mix[layer])
        hs = _norm3(ys, norm_mix[layer])
        if layer % 2 == 0:
            i = layer // 2
            w_in = ab_w_in[i]
            split = A_CONV_DIM + A_V_W
            w_small = jnp.pad(w_in[:, split:P_A], ((0, 0), (0, 128 - 2 * A_HEADS))).astype(bf16)
            w_in_parts = (w_in[:, :split].astype(bf16), w_small, w_in[:, P_A:].astype(bf16))
            w = (w_in_parts, ab_w_out[i].astype(bf16), a_conv_w[i], a_log[i], a_dt_bias[i], a_out_norm[i],
                 b_q_norm[i], b_k_norm[i])
            conv0 = jnp.zeros((bp, A_CONV - 1, A_CONV_DIM), x_prompt.dtype)
            s0 = jnp.zeros((bp, A_HEADS, A_DK, A_DV), jnp.float32)
            yp, kp, vp, cp, dp = ab_mixer(yp, hp, 0, conv0, s0, None, *w)
            ys, kn, vn, cn, dn = ab_mixer(ys, hs, past_len, state_conv[i], state_delta[i],
                                          (cache_k, cache_v, i, page_table), *w)
            kp_l.append(kp); vp_l.append(vp); dp_l.append(dp); cp_l.append(cp)
            ks_l.append(kn); vs_l.append(vn); ds_l.append(dn); cs_l.append(cn)
        else:
            j = layer // 2
            w = (c_w_in[j].astype(bf16), c_b_in[j], c_ln_g[j], c_ln_b[j], c_w_s[j], c_b_s[j], c_w_out[j].astype(bf16))
            yp, _ = c_mixer(yp, hp, *w)
            ys, sv = c_mixer(ys, hs, *w)
            sv_l.append(sv)
        pw = (peer_w_q[layer].astype(bf16), peer_sub_keys[layer], pack_tables(peer_u[layer], peer_v[layer]))
        yp = peer_layer(yp, norm_ffn[layer], *pw)
        ys = peer_layer(ys, norm_ffn[layer], *pw)
    return (yp, ys, jnp.stack(kp_l), jnp.stack(vp_l), jnp.stack(dp_l), jnp.stack(cp_l),
            jnp.stack(ks_l), jnp.stack(vs_l), jnp.stack(ds_l), jnp.stack(cs_l), jnp.stack(sv_l))
```
